```python
import jax, jax.numpy as jnp
from jax import lax
import numpy as np

D_MODEL = 2048
BATCH = 4
SEQ = 2048
DEPTH = 4

CHUNK = 64
N_MEM = 256
N_MEM_HEADS = 4
MEM_HEAD_DIM = D_MODEL // N_MEM_HEADS
D_FF = ((8 * D_MODEL // 3 + 255) // 256) * 256
A_WIDTH = D_MODEL // 2
B_WIDTH = D_MODEL // 2
A_HEADS = 8
A_HEAD_DIM = A_WIDTH // A_HEADS
GMLP_BLOCK = 128
CONV_WIDTH = 31
POOL_WINDOWS = (2, 4, 8, 16)
C_GROUPS = len(POOL_WINDOWS)
C_GROUP_DIM = D_MODEL // C_GROUPS
N_EVEN = (DEPTH + 1) // 2
N_ODD = DEPTH // 2
EPS = 1e-6

kernel_name = "hybrid_gmlp_conformer_pool_encoder"


def rms_norm(x, g):
    x32 = x.astype(jnp.float32)
    y = x32 * lax.rsqrt(jnp.mean(x32 * x32, axis=-1, keepdims=True) + EPS)
    return (y * g.astype(jnp.float32)).astype(x.dtype)


def layer_norm(x, g, b):
    x32 = x.astype(jnp.float32)
    mu = jnp.mean(x32, axis=-1, keepdims=True)
    xc = x32 - mu
    var = jnp.mean(xc * xc, axis=-1, keepdims=True)
    y = xc * lax.rsqrt(var + EPS)
    return (y * g.astype(jnp.float32) + b.astype(jnp.float32)).astype(x.dtype)


def swiglu_ffn(h, w_gate, w_up, w_down):
    return (jax.nn.silu(h @ w_gate) * (h @ w_up)) @ w_down


def gmlp_spatial_gate(u, v, w_s, b_s, ln_g, ln_b):
    b, s, _ = u.shape
    v = layer_norm(v, ln_g, ln_b)
    pos = jnp.arange(GMLP_BLOCK)
    mask = (pos[None, :] // CHUNK) <= (pos[:, None] // CHUNK)
    w = jnp.where(mask[None], w_s, jnp.zeros_like(w_s))
    vb = v.reshape(b, s // GMLP_BLOCK, GMLP_BLOCK, A_HEADS, A_HEAD_DIM)
    sp = jnp.einsum('hpq,bnqhc->bnphc', w, vb) + b_s.T[None, None, :, :, None]
    return u * sp.reshape(b, s, A_WIDTH)


def conformer_conv(a, g, conv_w, conv_b, ln_g, ln_b):
    h = a * jax.nn.sigmoid(g)
    h = lax.conv_general_dilated(
        h, conv_w, window_strides=(1,), padding=[(CONV_WIDTH - 1, 0)],
        dimension_numbers=('NWC', 'WIO', 'NWC'), feature_group_count=B_WIDTH) + conv_b
    h = layer_norm(h, ln_g, ln_b)
    return jax.nn.silu(h)


def even_mixer(h, w_in, b_in, w_s, b_s, gln_g, gln_b, conv_w, conv_b, cln_g, cln_b, w_out, b_out):
    z = h @ w_in + b_in
    u_a, v_a, a_b, g_b = jnp.split(z, [A_WIDTH, 2 * A_WIDTH, 2 * A_WIDTH + B_WIDTH], axis=-1)
    y_a = gmlp_spatial_gate(jax.nn.gelu(u_a), jax.nn.gelu(v_a), w_s, b_s, gln_g, gln_b)
    y_b = conformer_conv(a_b, g_b, conv_w, conv_b, cln_g, cln_b)
    return jnp.concatenate([y_a, y_b], axis=-1) @ w_out + b_out


def multiscale_pool_mixer(h, w_c, b_c, scale):
    b, s, _ = h.shape
    h32 = h.astype(jnp.float32)
    cs = jnp.cumsum(h32, axis=1)
    t = jnp.arange(1, s + 1, dtype=jnp.float32)[None, :, None]
    outs = []
    for gi, win in enumerate(POOL_WINDOWS):
        sl = slice(gi * C_GROUP_DIM, (gi + 1) * C_GROUP_DIM)
        c = cs[..., sl]
        prev = jnp.pad(c, ((0, 0), (win, 0), (0, 0)))[:, :s]
        mean = (c - prev) / jnp.minimum(t, win)
        d_g = (mean - h32[..., sl]).astype(h.dtype)
        outs.append(d_g @ w_c[gi] + b_c[gi])
    return jnp.concatenate(outs, axis=-1) * scale


def memory_cross_attention(h, mem_n, wq, wk, wv, wo):
    b, s, _ = h.shape
    m = mem_n.shape[1]
    q = (h @ wq).reshape(b, s, N_MEM_HEADS, MEM_HEAD_DIM)
    k = (mem_n @ wk).reshape(b, m, N_MEM_HEADS, MEM_HEAD_DIM)
    v = (mem_n @ wv).reshape(b, m, N_MEM_HEADS, MEM_HEAD_DIM)
    scores = jnp.einsum('bshd,bmhd->bhsm', q, k).astype(jnp.float32) * (MEM_HEAD_DIM ** -0.5)
    p = jax.nn.softmax(scores, axis=-1).astype(v.dtype)
    o = jnp.einsum('bhsm,bmhd->bshd', p, v).reshape(b, s, D_MODEL)
    return o @ wo


def setup_inputs(seed: int = 0) -> dict:
    key = jax.random.key(seed)
    ks = iter(jax.random.split(key, 48))

    def nrm(shape, scale):
        return jax.random.normal(next(ks), shape, jnp.float32) * scale

    def gain(shape):
        return 1.0 + nrm(shape, 0.1)

    D, F = D_MODEL, D_FF
    return {
        "x": nrm((BATCH, SEQ, D), 1.0),
        "mem": nrm((BATCH, N_MEM, D), 1.0),
        "norm_ffn1": gain((DEPTH, D)),
        "ffn1_gate": nrm((DEPTH, D, F), D ** -0.5),
        "ffn1_up": nrm((DEPTH, D, F), D ** -0.5),
        "ffn1_down": nrm((DEPTH, F, D), F ** -0.5),
        "norm_mix": gain((DEPTH, D)),
        "ab_w_in": nrm((N_EVEN, D, 2 * A_WIDTH + 2 * B_WIDTH), D ** -0.5),
        "ab_b_in": nrm((N_EVEN, 2 * A_WIDTH + 2 * B_WIDTH), 0.02),
        "gmlp_w_s": nrm((N_EVEN, A_HEADS, GMLP_BLOCK, GMLP_BLOCK), GMLP_BLOCK ** -0.5),
        "gmlp_b_s": gain((N_EVEN, A_HEADS, GMLP_BLOCK)),
        "gmlp_ln_g": gain((N_EVEN, A_WIDTH)),
        "gmlp_ln_b": nrm((N_EVEN, A_WIDTH), 0.02),
        "conv_w": nrm((N_EVEN, CONV_WIDTH, 1, B_WIDTH), CONV_WIDTH ** -0.5),
        "conv_b": nrm((N_EVEN, B_WIDTH), 0.02),
        "conv_ln_g": gain((N_EVEN, B_WIDTH)),
        "conv_ln_b": nrm((N_EVEN, B_WIDTH), 0.02),
        "ab_w_out": nrm((N_EVEN, A_WIDTH + B_WIDTH, D), (A_WIDTH + B_WIDTH) ** -0.5),
        "ab_b_out": nrm((N_EVEN, D), 0.02),
        "pool_w": nrm((N_ODD, C_GROUPS, C_GROUP_DIM, C_GROUP_DIM), C_GROUP_DIM ** -0.5),
        "pool_b": nrm((N_ODD, C_GROUPS, C_GROUP_DIM), 0.02),
        "pool_scale": 0.5 + nrm((N_ODD, D), 0.05),
        "norm_xq": gain((DEPTH, D)),
        "norm_xkv": gain((DEPTH, D)),
        "xattn_wq": nrm((DEPTH, D, D), D ** -0.5),
        "xattn_wk": nrm((DEPTH, D, D), D ** -0.5),
        "xattn_wv": nrm((DEPTH, D, D), D ** -0.5),
        "xattn_wo": nrm((DEPTH, D, D), D ** -0.5),
        "norm_ffn2": gain((DEPTH, D)),
        "ffn2_gate": nrm((DEPTH, D, F), D ** -0.5),
        "ffn2_up": nrm((DEPTH, D, F), D ** -0.5),
        "ffn2_down": nrm((DEPTH, F, D), F ** -0.5),
        "norm_final": gain((D,)),
    }


def reference(x, mem, norm_ffn1, ffn1_gate, ffn1_up, ffn1_down, norm_mix,
              ab_w_in, ab_b_in, gmlp_w_s, gmlp_b_s, gmlp_ln_g, gmlp_ln_b,
              conv_w, conv_b, conv_ln_g, conv_ln_b, ab_w_out, ab_b_out,
              pool_w, pool_b, pool_scale, norm_xq, norm_xkv,
              xattn_wq, xattn_wk, xattn_wv, xattn_wo,
              norm_ffn2, ffn2_gate, ffn2_up, ffn2_down, norm_final):
    for l in range(DEPTH):
        h = rms_norm(x, norm_ffn1[l])
        x = x + 0.5 * swiglu_ffn(h, ffn1_gate[l], ffn1_up[l], ffn1_down[l])
        h = rms_norm(x, norm_mix[l])
        if l % 2 == 0:
            e = l // 2
            x = x + even_mixer(h, ab_w_in[e], ab_b_in[e], gmlp_w_s[e], gmlp_b_s[e],
                               gmlp_ln_g[e], gmlp_ln_b[e], conv_w[e], conv_b[e],
                               conv_ln_g[e], conv_ln_b[e], ab_w_out[e], ab_b_out[e])
        else:
            o = l // 2
            x = x + multiscale_pool_mixer(h, pool_w[o], pool_b[o], pool_scale[o])
        h = rms_norm(x, norm_xq[l])
        m = rms_norm(mem, norm_xkv[l])
        x = x + memory_cross_attention(h, m, xattn_wq[l], xattn_wk[l], xattn_wv[l], xattn_wo[l])
        h = rms_norm(x, norm_ffn2[l])
        x = x + 0.5 * swiglu_ffn(h, ffn2_gate[l], ffn2_up[l], ffn2_down[l])
    return rms_norm(x, norm_final)
```

```python
import functools

import jax
import jax.numpy as jnp
from jax import lax
from jax.experimental import pallas as pl
from jax.experimental.pallas import tpu as pltpu

F32 = jnp.float32
BF16 = jnp.bfloat16

DEPTH = 4
CHUNK = 64
N_MEM = 256
N_MEM_HEADS = 4
A_HEADS = 8
GMLP_BLOCK = 128
CONV_WIDTH = 31
POOL_WINDOWS = (2, 4, 8, 16)
EPS = 1e-6

CONV_HALO = 32
POOL_HALO = 16

VMEM_LIMIT_BYTES = 58 * 1024 * 1024


def _params(*semantics):
    return pltpu.CompilerParams(dimension_semantics=semantics, vmem_limit_bytes=VMEM_LIMIT_BYTES)


def _rms(x, g):
    return x * lax.rsqrt(jnp.mean(x * x, axis=-1, keepdims=True) + EPS) * g


def _layer_norm(x, g, b):
    mu = jnp.mean(x, axis=-1, keepdims=True)
    xc = x - mu
    var = jnp.mean(xc * xc, axis=-1, keepdims=True)
    return xc * lax.rsqrt(var + EPS) * g + b


def _silu(x):
    return x * jax.nn.sigmoid(x)


def _norm_matmul_kernel(*refs, has_bias):
    if has_bias:
        x_ref, g_ref, w_ref, b_ref, o_ref, h_ref = refs
    else:
        x_ref, g_ref, w_ref, o_ref, h_ref = refs

    @pl.when(pl.program_id(1) == 0)
    def _():
        h_ref[...] = _rms(x_ref[...], g_ref[...]).astype(BF16)

    acc = jnp.dot(h_ref[...], w_ref[...], preferred_element_type=F32)
    if has_bias:
        acc = acc + b_ref[...]
    o_ref[...] = acc.astype(o_ref.dtype)


def _norm_matmul(x, gain, w, bias, out_dtype, tm, tn):
    t, k = x.shape
    g_arr, gl = gain
    w_arr, wl = w
    n = w_arr.shape[-1]
    in_specs = [
        pl.BlockSpec((tm, k), lambda i, j: (i, 0)),
        pl.BlockSpec((None, 1, k), lambda i, j: (gl, 0, 0)),
        pl.BlockSpec((None, k, tn), lambda i, j: (wl, 0, j)),
    ]
    args = [x, g_arr, w_arr]
    if bias is not None:
        b_arr, bl = bias
        in_specs.append(pl.BlockSpec((None, 1, tn), lambda i, j: (bl, 0, j)))
        args.append(b_arr)
    return pl.pallas_call(
        functools.partial(_norm_matmul_kernel, has_bias=bias is not None),
        grid=(t // tm, n // tn),
        in_specs=in_specs,
        out_specs=pl.BlockSpec((tm, tn), lambda i, j: (i, j)),
        out_shape=jax.ShapeDtypeStruct((t, n), out_dtype),
        scratch_shapes=[pltpu.VMEM((tm, k), BF16)],
        compiler_params=_params("parallel", "arbitrary"),
        name="norm_matmul",
    )(*args)


def _norm_swiglu_kernel(x_ref, g_ref, wg_ref, wu_ref, o_ref, h_ref):
    @pl.when(pl.program_id(1) == 0)
    def _():
        h_ref[...] = _rms(x_ref[...], g_ref[...]).astype(BF16)

    h = h_ref[...]
    gate = jnp.dot(h, wg_ref[...], preferred_element_type=F32)
    up = jnp.dot(h, wu_ref[...], preferred_element_type=F32)
    o_ref[...] = (_silu(gate) * up).astype(o_ref.dtype)


def _norm_swiglu(x, gain, wg, wu, layer, tm, tn):
    t, k = x.shape
    g_arr, gl = gain
    f = wg.shape[-1]
    return pl.pallas_call(
        _norm_swiglu_kernel,
        grid=(t // tm, f // tn),
        in_specs=[
            pl.BlockSpec((tm, k), lambda i, j: (i, 0)),
            pl.BlockSpec((None, 1, k), lambda i, j: (gl, 0, 0)),
            pl.BlockSpec((None, k, tn), lambda i, j: (layer, 0, j)),
            pl.BlockSpec((None, k, tn), lambda i, j: (layer, 0, j)),
        ],
        out_specs=pl.BlockSpec((tm, tn), lambda i, j: (i, j)),
        out_shape=jax.ShapeDtypeStruct((t, f), BF16),
        scratch_shapes=[pltpu.VMEM((tm, k), BF16)],
        compiler_params=_params("parallel", "arbitrary"),
        name="norm_swiglu",
    )(x, g_arr, wg, wu)


def _matmul_residual_kernel(*refs, has_bias, scale):
    if has_bias:
        a_ref, w_ref, b_ref, r_ref, o_ref = refs
    else:
        a_ref, w_ref, r_ref, o_ref = refs
    acc = jnp.dot(a_ref[...], w_ref[...], preferred_element_type=F32)
    if has_bias:
        acc = acc + b_ref[...]
    if scale != 1.0:
        acc = scale * acc
    o_ref[...] = r_ref[...] + acc


def _matmul_residual(a, w, bias, res, scale, tm, tn):
    t, k = a.shape
    w_arr, wl = w
    n = w_arr.shape[-1]
    in_specs = [
        pl.BlockSpec((tm, k), lambda i, j: (i, 0)),
        pl.BlockSpec((None, k, tn), lambda i, j: (wl, 0, j)),
    ]
    args = [a, w_arr]
    if bias is not None:
        b_arr, bl = bias
        in_specs.append(pl.BlockSpec((None, 1, tn), lambda i, j: (bl, 0, j)))
        args.append(b_arr)
    in_specs.append(pl.BlockSpec((tm, tn), lambda i, j: (i, j)))
    args.append(res)
    return pl.pallas_call(
        functools.partial(_matmul_residual_kernel, has_bias=bias is not None, scale=scale),
        grid=(t // tm, n // tn),
        in_specs=in_specs,
        out_specs=pl.BlockSpec((tm, tn), lambda i, j: (i, j)),
        out_shape=jax.ShapeDtypeStruct((t, n), F32),
        compiler_params=_params("parallel", "arbitrary"),
        name="matmul_residual",
    )(*args)


def _attn_kernel(q_ref, k_ref, v_ref, o_ref, *, head_dim):
    scale = head_dim ** -0.5
    for h in range(N_MEM_HEADS):
        sl = slice(h * head_dim, (h + 1) * head_dim)
        s = lax.dot_general(q_ref[:, sl], k_ref[:, sl], (((1,), (1,)), ((), ())),
                            preferred_element_type=F32) * scale
        e = jnp.exp(s - jnp.max(s, axis=-1, keepdims=True))
        p = e / jnp.sum(e, axis=-1, keepdims=True)
        o_ref[:, sl] = jnp.dot(p.astype(BF16), v_ref[:, sl],
                               preferred_element_type=F32).astype(o_ref.dtype)


def _attention(q, k, v, batch, ts):
    t, d = q.shape
    m = k.shape[0] // batch
    tiles = t // batch // ts
    return pl.pallas_call(
        functools.partial(_attn_kernel, head_dim=d // N_MEM_HEADS),
        grid=(batch, tiles),
        in_specs=[
            pl.BlockSpec((ts, d), lambda b, i: (b * tiles + i, 0)),
            pl.BlockSpec((m, d), lambda b, i: (b, 0)),
            pl.BlockSpec((m, d), lambda b, i: (b, 0)),
        ],
        out_specs=pl.BlockSpec((ts, d), lambda b, i: (b * tiles + i, 0)),
        out_shape=jax.ShapeDtypeStruct((t, d), BF16),
        compiler_params=_params("parallel", "arbitrary"),
        name="attention",
    )(q, k, v)


def _even_mixer_kernel(z_ref, zh_ref, ws_ref, bs_ref, glg_ref, glb_ref, cw_ref, cb_ref,
                       clg_ref, clb_ref, y_ref, hh_ref, *, ts, width):
    i = pl.program_id(1)
    hd = width // A_HEADS

    u = jax.nn.gelu(z_ref[:, 0:width])
    v = jax.nn.gelu(z_ref[:, width:2 * width])
    vn = _layer_norm(v, glg_ref[...], glb_ref[...]).astype(BF16)
    row_chunk = lax.broadcasted_iota(jnp.int32, (GMLP_BLOCK, GMLP_BLOCK), 0) // CHUNK
    col_chunk = lax.broadcasted_iota(jnp.int32, (GMLP_BLOCK, GMLP_BLOCK), 1) // CHUNK
    causal = col_chunk <= row_chunk
    nblk = ts // GMLP_BLOCK
    for h in range(A_HEADS):
        cs = slice(h * hd, (h + 1) * hd)
        w = jnp.where(causal, ws_ref[h], 0.0).astype(BF16)
        vh = jnp.concatenate(
            [vn[n * GMLP_BLOCK:(n + 1) * GMLP_BLOCK, cs] for n in range(nblk)], axis=1)
        sp = jnp.dot(w, vh, preferred_element_type=F32) + bs_ref[:, h:h + 1]
        for n in range(nblk):
            rs = slice(n * GMLP_BLOCK, (n + 1) * GMLP_BLOCK)
            y_ref[rs, cs] = (u[rs, cs] * sp[:, n * hd:(n + 1) * hd]).astype(y_ref.dtype)

    hh_ref[CONV_HALO:, :] = z_ref[:, 2 * width:3 * width] * jax.nn.sigmoid(z_ref[:, 3 * width:4 * width])
    halo = zh_ref[:, 0:width] * jax.nn.sigmoid(zh_ref[:, width:2 * width])
    hh_ref[0:CONV_HALO, :] = jnp.where(i == 0, 0.0, halo)
    first = CONV_HALO - (CONV_WIDTH - 1)
    acc = cw_ref[0:1, :] * hh_ref[pl.ds(first, ts), :]
    for kk in range(1, CONV_WIDTH):
        acc = acc + cw_ref[kk:kk + 1, :] * hh_ref[pl.ds(first + kk, ts), :]
    acc = acc + cb_ref[...]
    y_ref[:, width:2 * width] = _silu(_layer_norm(acc, clg_ref[...], clb_ref[...])).astype(y_ref.dtype)


def _even_mixer(z, e, w_s, b_s_t, gln_g, gln_b, conv_w, conv_b, cln_g, cln_b, batch, ts):
    _, s, zw = z.shape
    width = zw // 4
    tiles = s // ts
    halo_per_tile = ts // CONV_HALO
    vec = lambda: pl.BlockSpec((None, 1, width), lambda b, i: (e, 0, 0))
    return pl.pallas_call(
        functools.partial(_even_mixer_kernel, ts=ts, width=width),
        grid=(batch, tiles),
        in_specs=[
            pl.BlockSpec((None, ts, zw), lambda b, i: (b, i, 0)),
            pl.BlockSpec((None, CONV_HALO, 2 * width),
                         lambda b, i: (b, jnp.maximum(i * halo_per_tile - 1, 0), 1)),
            pl.BlockSpec((None, A_HEADS, GMLP_BLOCK, GMLP_BLOCK), lambda b, i: (e, 0, 0, 0)),
            pl.BlockSpec((None, GMLP_BLOCK, A_HEADS), lambda b, i: (e, 0, 0)),
            vec(), vec(),
            pl.BlockSpec((None, CONV_WIDTH, width), lambda b, i: (e, 0, 0)),
            vec(), vec(), vec(),
        ],
        out_specs=pl.BlockSpec((None, ts, 2 * width), lambda b, i: (b, i, 0)),
        out_shape=jax.ShapeDtypeStruct((batch, s, 2 * width), BF16),
        scratch_shapes=[pltpu.VMEM((ts + CONV_HALO, width), F32)],
        compiler_params=_params("parallel", "arbitrary"),
        name="even_mixer",
    )(z, z, w_s, b_s_t, gln_g, gln_b, conv_w, conv_b, cln_g, cln_b)


def _pool_mixer_kernel(x_ref, xh_ref, g_ref, w_ref, b_ref, sc_ref, o_ref, *, ts, group_dim):
    i = pl.program_id(1)
    g = g_ref[...]
    x = x_ref[...]
    h = _rms(x, g)
    h_halo = jnp.where(i == 0, 0.0, _rms(xh_ref[...], g))
    hist = jnp.concatenate([h_halo, h], axis=0)
    frames = i * ts + lax.broadcasted_iota(jnp.int32, (ts, 1), 0) + 1
    for gi, win in enumerate(POOL_WINDOWS):
        cs = slice(gi * group_dim, (gi + 1) * group_dim)
        wsum = hist[:, cs]
        span = 1
        while span < win:
            wsum = wsum + pltpu.roll(wsum, span, axis=0)
            span *= 2
        mean = wsum[POOL_HALO:, :] / jnp.minimum(frames, win).astype(F32)
        d = (mean - h[:, cs]).astype(BF16)
        out = jnp.dot(d, w_ref[gi], preferred_element_type=F32) + b_ref[:, cs]
        o_ref[:, cs] = x[:, cs] + out * sc_ref[:, cs]


def _pool_mixer(x, gain, o, pool_w, pool_b, pool_scale, batch, ts):
    _, s, d = x.shape
    g_arr, gl = gain
    groups = len(POOL_WINDOWS)
    group_dim = d // groups
    tiles = s // ts
    halo_per_tile = ts // POOL_HALO
    return pl.pallas_call(
        functools.partial(_pool_mixer_kernel, ts=ts, group_dim=group_dim),
        grid=(batch, tiles),
        in_specs=[
            pl.BlockSpec((None, ts, d), lambda b, i: (b, i, 0)),
            pl.BlockSpec((None, POOL_HALO, d), lambda b, i: (b, jnp.maximum(i * halo_per_tile - 1, 0), 0)),
            pl.BlockSpec((None, 1, d), lambda b, i: (gl, 0, 0)),
            pl.BlockSpec((None, groups, group_dim, group_dim), lambda b, i: (o, 0, 0, 0)),
            pl.BlockSpec((None, 1, d), lambda b, i: (o, 0, 0)),
            pl.BlockSpec((None, 1, d), lambda b, i: (o, 0, 0)),
        ],
        out_specs=pl.BlockSpec((None, ts, d), lambda b, i: (b, i, 0)),
        out_shape=jax.ShapeDtypeStruct(x.shape, F32),
        compiler_params=_params("parallel", "arbitrary"),
        name="pool_mixer",
    )(x, x, g_arr, pool_w, pool_b, pool_scale)


def _rmsnorm_kernel(x_ref, g_ref, o_ref):
    o_ref[...] = _rms(x_ref[...], g_ref[...])


def _rmsnorm(x, g, tm):
    t, d = x.shape
    return pl.pallas_call(
        _rmsnorm_kernel,
        grid=(t // tm,),
        in_specs=[pl.BlockSpec((tm, d), lambda i: (i, 0)), pl.BlockSpec((1, d), lambda i: (0, 0))],
        out_specs=pl.BlockSpec((tm, d), lambda i: (i, 0)),
        out_shape=jax.ShapeDtypeStruct((t, d), F32),
        compiler_params=_params("parallel"),
        name="final_rmsnorm",
    )(x, g)


def kernel(x, mem, norm_ffn1, ffn1_gate, ffn1_up, ffn1_down, norm_mix, ab_w_in, ab_b_in, gmlp_w_s, gmlp_b_s, gmlp_ln_g, gmlp_ln_b, conv_w, conv_b, conv_ln_g, conv_ln_b, ab_w_out, ab_b_out, pool_w, pool_b, pool_scale, norm_xq, norm_xkv, xattn_wq, xattn_wk, xattn_wv, xattn_wo, norm_ffn2, ffn2_gate, ffn2_up, ffn2_down, norm_final):
    batch, seq, d = x.shape
    t = batch * seq
    row = lambda a: a.reshape(a.shape[0], 1, a.shape[-1])
    bf = lambda a: a.astype(BF16)

    norm_ffn1, norm_mix, norm_xq, norm_xkv, norm_ffn2 = map(row, (norm_ffn1, norm_mix, norm_xq, norm_xkv, norm_ffn2))
    ffn1 = (bf(ffn1_gate), bf(ffn1_up), bf(ffn1_down))
    ffn2 = (bf(ffn2_gate), bf(ffn2_up), bf(ffn2_down))
    ab_w_in, ab_w_out, pool_w = bf(ab_w_in), bf(ab_w_out), bf(pool_w)
    wq, wk, wv, wo = bf(xattn_wq), bf(xattn_wk), bf(xattn_wv), bf(xattn_wo)
    ab_b_in, ab_b_out, pool_scale = row(ab_b_in), row(ab_b_out), row(pool_scale)
    pool_b = pool_b.reshape(pool_b.shape[0], 1, d)
    gmlp_b_s_t = jnp.swapaxes(gmlp_b_s, 1, 2)
    gmlp_ln_g, gmlp_ln_b, conv_b, conv_ln_g, conv_ln_b = map(row, (gmlp_ln_g, gmlp_ln_b, conv_b, conv_ln_g, conv_ln_b))
    conv_w = conv_w.reshape(conv_w.shape[0], CONV_WIDTH, conv_w.shape[-1])

    xf = x.reshape(t, d)
    memf = mem.reshape(batch * N_MEM, d)

    def ffn(xf, gain, weights, layer):
        w_gate, w_up, w_down = weights
        hidden = _norm_swiglu(xf, gain, w_gate, w_up, layer, tm=1024, tn=512)
        return _matmul_residual(hidden, (w_down, layer), None, xf, 0.5, tm=1024, tn=512)

    for l in range(DEPTH):
        xf = ffn(xf, (norm_ffn1, l), ffn1, l)
        if l % 2 == 0:
            e = l // 2
            z = _norm_matmul(xf, (norm_mix, l), (ab_w_in, e), (ab_b_in, e), F32, tm=1024, tn=1024)
            y = _even_mixer(z.reshape(batch, seq, -1), e, gmlp_w_s, gmlp_b_s_t, gmlp_ln_g, gmlp_ln_b,
                            conv_w, conv_b, conv_ln_g, conv_ln_b, batch, ts=512)
            xf = _matmul_residual(y.reshape(t, d), (ab_w_out, e), (ab_b_out, e), xf, 1.0, tm=1024, tn=1024)
        else:
            o = l // 2
            xf = _pool_mixer(xf.reshape(batch, seq, d), (norm_mix, l), o, pool_w, pool_b, pool_scale,
                             batch, ts=512).reshape(t, d)
        q = _norm_matmul(xf, (norm_xq, l), (wq, l), None, BF16, tm=1024, tn=1024)
        k = _norm_matmul(memf, (norm_xkv, l), (wk, l), None, BF16, tm=batch * N_MEM, tn=1024)
        v = _norm_matmul(memf, (norm_xkv, l), (wv, l), None, BF16, tm=batch * N_MEM, tn=1024)
        att = _attention(q, k, v, batch, ts=1024)
        xf = _matmul_residual(att, (wo, l), None, xf, 1.0, tm=1024, tn=1024)
        xf = ffn(xf, (norm_ffn2, l), ffn2, l)
    return _rmsnorm(xf, norm_final.reshape(1, d), tm=1024).reshape(batch, seq, d)
```

```python
import functools

import jax
import jax.numpy as jnp
from jax import lax
from jax.experimental import pallas as pl
from jax.experimental.pallas import tpu as pltpu

F32 = jnp.float32
BF16 = jnp.bfloat16

DEPTH = 4
CHUNK = 64
N_MEM = 256
N_MEM_HEADS = 4
A_HEADS = 8
GMLP_BLOCK = 128
CONV_WIDTH = 31
POOL_WINDOWS = (2, 4, 8, 16)
EPS = 1e-6

CONV_HALO = 32
POOL_HALO = 16

VMEM_LIMIT_BYTES = 58 * 1024 * 1024


def _params(*semantics):
    return pltpu.CompilerParams(dimension_semantics=semantics, vmem_limit_bytes=VMEM_LIMIT_BYTES)


def _rms(x, g):
    return x * lax.rsqrt(jnp.mean(x * x, axis=-1, keepdims=True) + EPS) * g


def _layer_norm(x, g, b):
    mu = jnp.mean(x, axis=-1, keepdims=True)
    xc = x - mu
    var = jnp.mean(xc * xc, axis=-1, keepdims=True)
    return xc * lax.rsqrt(var + EPS) * g + b


def _silu(x):
    return x * jax.nn.sigmoid(x)


def _mxu(a, w_ref):
    return jnp.dot(a, w_ref[...].astype(BF16), preferred_element_type=F32)


def _norm_matmul_kernel(*refs, n_w, has_bias):
    x_ref, g_ref = refs[:2]
    w_refs = refs[2:2 + n_w]
    b_ref = refs[2 + n_w] if has_bias else None
    o_refs = refs[2 + n_w + has_bias:2 + 2 * n_w + has_bias]
    h_ref = refs[-1]

    @pl.when(pl.program_id(1) == 0)
    def _():
        h_ref[...] = _rms(x_ref[...], g_ref[...]).astype(BF16)

    h = h_ref[...]
    for w_ref, o_ref in zip(w_refs, o_refs):
        acc = _mxu(h, w_ref)
        if has_bias:
            acc = acc + b_ref[...]
        o_ref[...] = acc.astype(o_ref.dtype)


def _norm_matmul(x, gain, ws, bias, out_dtype, tm, tn):
    t, k = x.shape
    g_arr, gl = gain
    n = ws[0][0].shape[-1]
    in_specs = [
        pl.BlockSpec((tm, k), lambda i, j: (i, 0)),
        pl.BlockSpec((None, 1, k), lambda i, j: (gl, 0, 0)),
    ]
    args = [x, g_arr]
    for w_arr, wl in ws:
        in_specs.append(pl.BlockSpec((None, k, tn), lambda i, j, wl=wl: (wl, 0, j)))
        args.append(w_arr)
    if bias is not None:
        b_arr, bl = bias
        in_specs.append(pl.BlockSpec((None, 1, tn), lambda i, j: (bl, 0, j)))
        args.append(b_arr)
    outs = pl.pallas_call(
        functools.partial(_norm_matmul_kernel, n_w=len(ws), has_bias=bias is not None),
        grid=(t // tm, n // tn),
        in_specs=in_specs,
        out_specs=[pl.BlockSpec((tm, tn), lambda i, j: (i, j)) for _ in ws],
        out_shape=[jax.ShapeDtypeStruct((t, n), out_dtype) for _ in ws],
        scratch_shapes=[pltpu.VMEM((tm, k), BF16)],
        compiler_params=_params("parallel", "arbitrary"),
        name="norm_matmul",
    )(*args)
    return outs


def _norm_swiglu_kernel(x_ref, g_ref, wg_ref, wu_ref, o_ref, h_ref):
    @pl.when(pl.program_id(1) == 0)
    def _():
        h_ref[...] = _rms(x_ref[...], g_ref[...]).astype(BF16)

    h = h_ref[...]
    gate = _mxu(h, wg_ref)
    up = _mxu(h, wu_ref)
    o_ref[...] = (_silu(gate) * up).astype(o_ref.dtype)


def _norm_swiglu(x, gain, wg, wu, layer, tm, tn):
    t, k = x.shape
    g_arr, gl = gain
    f = wg.shape[-1]
    return pl.pallas_call(
        _norm_swiglu_kernel,
        grid=(t // tm, f // tn),
        in_specs=[
            pl.BlockSpec((tm, k), lambda i, j: (i, 0)),
            pl.BlockSpec((None, 1, k), lambda i, j: (gl, 0, 0)),
            pl.BlockSpec((None, k, tn), lambda i, j: (layer, 0, j)),
            pl.BlockSpec((None, k, tn), lambda i, j: (layer, 0, j)),
        ],
        out_specs=pl.BlockSpec((tm, tn), lambda i, j: (i, j)),
        out_shape=jax.ShapeDtypeStruct((t, f), BF16),
        scratch_shapes=[pltpu.VMEM((tm, k), BF16)],
        compiler_params=_params("parallel", "arbitrary"),
        name="norm_swiglu",
    )(x, g_arr, wg, wu)


def _matmul_residual_kernel(a_ref, w_ref, r_ref, o_ref, *, scale):
    o_ref[...] = r_ref[...] + scale * _mxu(a_ref[...], w_ref)


def _matmul_residual(a, w, res, scale, tm, tn):
    t, k = a.shape
    w_arr, wl = w
    n = w_arr.shape[-1]
    return pl.pallas_call(
        functools.partial(_matmul_residual_kernel, scale=scale),
        grid=(t // tm, n // tn),
        in_specs=[
            pl.BlockSpec((tm, k), lambda i, j: (i, 0)),
            pl.BlockSpec((None, k, tn), lambda i, j: (wl, 0, j)),
            pl.BlockSpec((tm, tn), lambda i, j: (i, j)),
        ],
        out_specs=pl.BlockSpec((tm, tn), lambda i, j: (i, j)),
        out_shape=jax.ShapeDtypeStruct((t, n), F32),
        compiler_params=_params("parallel", "arbitrary"),
        name="matmul_residual",
    )(a, w_arr, res)


def _proj_residual_kernel(*refs, has_bias):
    if has_bias:
        a_ref, w_ref, b_ref, r_ref, o_ref, wb_ref = refs
    else:
        a_ref, w_ref, r_ref, o_ref, wb_ref = refs

    @pl.when(pl.program_id(0) == 0)
    def _():
        wb_ref[...] = w_ref[...].astype(BF16)

    acc = jnp.dot(a_ref[...], wb_ref[...], preferred_element_type=F32)
    if has_bias:
        acc = acc + b_ref[...]
    o_ref[...] = r_ref[...] + acc


def _proj_residual(a, w, bias, res, tm):
    t, k = a.shape
    w_arr, wl = w
    n = w_arr.shape[-1]
    in_specs = [
        pl.BlockSpec((tm, k), lambda i: (i, 0)),
        pl.BlockSpec((None, k, n), lambda i: (wl, 0, 0), pipeline_mode=pl.Buffered(1)),
    ]
    args = [a, w_arr]
    if bias is not None:
        b_arr, bl = bias
        in_specs.append(pl.BlockSpec((None, 1, n), lambda i: (bl, 0, 0)))
        args.append(b_arr)
    in_specs.append(pl.BlockSpec((tm, n), lambda i: (i, 0)))
    args.append(res)
    return pl.pallas_call(
        functools.partial(_proj_residual_kernel, has_bias=bias is not None),
        grid=(t // tm,),
        in_specs=in_specs,
        out_specs=pl.BlockSpec((tm, n), lambda i: (i, 0)),
        out_shape=jax.ShapeDtypeStruct((t, n), F32),
        scratch_shapes=[pltpu.VMEM((k, n), BF16)],
        compiler_params=_params("arbitrary"),
        name="proj_residual",
    )(*args)


def _attn_kernel(q_ref, k_ref, v_ref, o_ref, *, head_dim):
    scale = head_dim ** -0.5
    for h in range(N_MEM_HEADS):
        sl = slice(h * head_dim, (h + 1) * head_dim)
        s = lax.dot_general(q_ref[:, sl], k_ref[:, sl], (((1,), (1,)), ((), ())),
                            preferred_element_type=F32) * scale
        e = jnp.exp(s - jnp.max(s, axis=-1, keepdims=True))
        p = e / jnp.sum(e, axis=-1, keepdims=True)
        o_ref[:, sl] = jnp.dot(p.astype(BF16), v_ref[:, sl],
                               preferred_element_type=F32).astype(o_ref.dtype)


def _attention(q, k, v, batch, ts):
    t, d = q.shape
    m = k.shape[0] // batch
    tiles = t // batch // ts
    return pl.pallas_call(
        functools.partial(_attn_kernel, head_dim=d // N_MEM_HEADS),
        grid=(batch, tiles),
        in_specs=[
            pl.BlockSpec((ts, d), lambda b, i: (b * tiles + i, 0)),
            pl.BlockSpec((m, d), lambda b, i: (b, 0)),
            pl.BlockSpec((m, d), lambda b, i: (b, 0)),
        ],
        out_specs=pl.BlockSpec((ts, d), lambda b, i: (b * tiles + i, 0)),
        out_shape=jax.ShapeDtypeStruct((t, d), BF16),
        compiler_params=_params("parallel", "arbitrary"),
        name="attention",
    )(q, k, v)


def _even_mixer_kernel(z_ref, zh_ref, ws_ref, bs_ref, glg_ref, glb_ref, cw_ref, cb_ref,
                       clg_ref, clb_ref, y_ref, hh_ref, *, ts, width):
    i = pl.program_id(1)
    hd = width // A_HEADS

    u = jax.nn.gelu(z_ref[:, 0:width])
    v = jax.nn.gelu(z_ref[:, width:2 * width])
    vn = _layer_norm(v, glg_ref[...], glb_ref[...]).astype(BF16)
    row_chunk = lax.broadcasted_iota(jnp.int32, (GMLP_BLOCK, GMLP_BLOCK), 0) // CHUNK
    col_chunk = lax.broadcasted_iota(jnp.int32, (GMLP_BLOCK, GMLP_BLOCK), 1) // CHUNK
    causal = col_chunk <= row_chunk
    nblk = ts // GMLP_BLOCK
    for h in range(A_HEADS):
        cs = slice(h * hd, (h + 1) * hd)
        w = jnp.where(causal, ws_ref[h], 0.0).astype(BF16)
        vh = jnp.concatenate(
            [vn[n * GMLP_BLOCK:(n + 1) * GMLP_BLOCK, cs] for n in range(nblk)], axis=1)
        sp = jnp.dot(w, vh, preferred_element_type=F32) + bs_ref[:, h:h + 1]
        for n in range(nblk):
            rs = slice(n * GMLP_BLOCK, (n + 1) * GMLP_BLOCK)
            y_ref[rs, cs] = (u[rs, cs] * sp[:, n * hd:(n + 1) * hd]).astype(y_ref.dtype)

    hh_ref[CONV_HALO:, :] = z_ref[:, 2 * width:3 * width] * jax.nn.sigmoid(z_ref[:, 3 * width:4 * width])
    halo = zh_ref[:, 0:width] * jax.nn.sigmoid(zh_ref[:, width:2 * width])
    hh_ref[0:CONV_HALO, :] = jnp.where(i == 0, 0.0, halo)
    first = CONV_HALO - (CONV_WIDTH - 1)
    acc = cw_ref[0:1, :] * hh_ref[pl.ds(first, ts), :]
    for kk in range(1, CONV_WIDTH):
        acc = acc + cw_ref[kk:kk + 1, :] * hh_ref[pl.ds(first + kk, ts), :]
    acc = acc + cb_ref[...]
    y_ref[:, width:2 * width] = _silu(_layer_norm(acc, clg_ref[...], clb_ref[...])).astype(y_ref.dtype)


def _even_mixer(z, e, w_s, b_s_t, gln_g, gln_b, conv_w, conv_b, cln_g, cln_b, batch, ts):
    _, s, zw = z.shape
    width = zw // 4
    tiles = s // ts
    halo_per_tile = ts // CONV_HALO
    vec = lambda: pl.BlockSpec((None, 1, width), lambda b, i: (e, 0, 0))
    return pl.pallas_call(
        functools.partial(_even_mixer_kernel, ts=ts, width=width),
        grid=(batch, tiles),
        in_specs=[
            pl.BlockSpec((None, ts, zw), lambda b, i: (b, i, 0)),
            pl.BlockSpec((None, CONV_HALO, 2 * width),
                         lambda b, i: (b, jnp.maximum(i * halo_per_tile - 1, 0), 1)),
            pl.BlockSpec((None, A_HEADS, GMLP_BLOCK, GMLP_BLOCK), lambda b, i: (e, 0, 0, 0)),
            pl.BlockSpec((None, GMLP_BLOCK, A_HEADS), lambda b, i: (e, 0, 0)),
            vec(), vec(),
            pl.BlockSpec((None, CONV_WIDTH, width), lambda b, i: (e, 0, 0)),
            vec(), vec(), vec(),
        ],
        out_specs=pl.BlockSpec((None, ts, 2 * width), lambda b, i: (b, i, 0)),
        out_shape=jax.ShapeDtypeStruct((batch, s, 2 * width), BF16),
        scratch_shapes=[pltpu.VMEM((ts + CONV_HALO, width), F32)],
        compiler_params=_params("parallel", "arbitrary"),
        name="even_mixer",
    )(z, z, w_s, b_s_t, gln_g, gln_b, conv_w, conv_b, cln_g, cln_b)


def _pool_mixer_kernel(x_ref, xh_ref, g_ref, w_ref, b_ref, sc_ref, o_ref, *, ts, group_dim):
    i = pl.program_id(1)
    g = g_ref[...]
    x = x_ref[...]
    h = _rms(x, g)
    h_halo = jnp.where(i == 0, 0.0, _rms(xh_ref[...], g))
    hist = jnp.concatenate([h_halo, h], axis=0)
    frames = i * ts + lax.broadcasted_iota(jnp.int32, (ts, 1), 0) + 1
    for gi, win in enumerate(POOL_WINDOWS):
        cs = slice(gi * group_dim, (gi + 1) * group_dim)
        wsum = hist[:, cs]
        span = 1
        while span < win:
            wsum = wsum + pltpu.roll(wsum, span, axis=0)
            span *= 2
        mean = wsum[POOL_HALO:, :] / jnp.minimum(frames, win).astype(F32)
        d = (mean - h[:, cs]).astype(BF16)
        out = jnp.dot(d, w_ref[gi].astype(BF16), preferred_element_type=F32) + b_ref[:, cs]
        o_ref[:, cs] = x[:, cs] + out * sc_ref[:, cs]


def _pool_mixer(x, gain, o, pool_w, pool_b, pool_scale, batch, ts):
    _, s, d = x.shape
    g_arr, gl = gain
    groups = len(POOL_WINDOWS)
    group_dim = d // groups
    tiles = s // ts
    halo_per_tile = ts // POOL_HALO
    return pl.pallas_call(
        functools.partial(_pool_mixer_kernel, ts=ts, group_dim=group_dim),
        grid=(batch, tiles),
        in_specs=[
            pl.BlockSpec((None, ts, d), lambda b, i: (b, i, 0)),
            pl.BlockSpec((None, POOL_HALO, d), lambda b, i: (b, jnp.maximum(i * halo_per_tile - 1, 0), 0)),
            pl.BlockSpec((None, 1, d), lambda b, i: (gl, 0, 0)),
            pl.BlockSpec((None, groups, group_dim, group_dim), lambda b, i: (o, 0, 0, 0)),
            pl.BlockSpec((None, 1, d), lambda b, i: (o, 0, 0)),
            pl.BlockSpec((None, 1, d), lambda b, i: (o, 0, 0)),
        ],
        out_specs=pl.BlockSpec((None, ts, d), lambda b, i: (b, i, 0)),
        out_shape=jax.ShapeDtypeStruct(x.shape, F32),
        compiler_params=_params("parallel", "arbitrary"),
        name="pool_mixer",
    )(x, x, g_arr, pool_w, pool_b, pool_scale)


def _rmsnorm_kernel(x_ref, g_ref, o_ref):
    o_ref[...] = _rms(x_ref[...], g_ref[...])


def _rmsnorm(x, g, tm):
    t, d = x.shape
    return pl.pallas_call(
        _rmsnorm_kernel,
        grid=(t // tm,),
        in_specs=[pl.BlockSpec((tm, d), lambda i: (i, 0)), pl.BlockSpec((1, d), lambda i: (0, 0))],
        out_specs=pl.BlockSpec((tm, d), lambda i: (i, 0)),
        out_shape=jax.ShapeDtypeStruct((t, d), F32),
        compiler_params=_params("parallel"),
        name="final_rmsnorm",
    )(x, g)


def kernel(x, mem, norm_ffn1, ffn1_gate, ffn1_up, ffn1_down, norm_mix, ab_w_in, ab_b_in, gmlp_w_s, gmlp_b_s, gmlp_ln_g, gmlp_ln_b, conv_w, conv_b, conv_ln_g, conv_ln_b, ab_w_out, ab_b_out, pool_w, pool_b, pool_scale, norm_xq, norm_xkv, xattn_wq, xattn_wk, xattn_wv, xattn_wo, norm_ffn2, ffn2_gate, ffn2_up, ffn2_down, norm_final):
    batch, seq, d = x.shape
    t = batch * seq
    row = lambda a: a.reshape(a.shape[0], 1, a.shape[-1])

    norm_ffn1, norm_mix, norm_xq, norm_xkv, norm_ffn2 = map(row, (norm_ffn1, norm_mix, norm_xq, norm_xkv, norm_ffn2))
    ab_b_in, ab_b_out, pool_scale = row(ab_b_in), row(ab_b_out), row(pool_scale)
    pool_b = pool_b.reshape(pool_b.shape[0], 1, d)
    gmlp_b_s_t = jnp.swapaxes(gmlp_b_s, 1, 2)
    gmlp_ln_g, gmlp_ln_b, conv_b, conv_ln_g, conv_ln_b = map(row, (gmlp_ln_g, gmlp_ln_b, conv_b, conv_ln_g, conv_ln_b))
    conv_w = conv_w.reshape(conv_w.shape[0], CONV_WIDTH, conv_w.shape[-1])

    xf = x.reshape(t, d)
    memf = mem.reshape(batch * N_MEM, d)

    def ffn(xf, gain, w_gate, w_up, w_down, layer):
        hidden = _norm_swiglu(xf, gain, w_gate, w_up, layer, tm=1024, tn=512)
        return _matmul_residual(hidden, (w_down, layer), xf, 0.5, tm=1024, tn=256)

    for l in range(DEPTH):
        xf = ffn(xf, (norm_ffn1, l), ffn1_gate, ffn1_up, ffn1_down, l)
        if l % 2 == 0:
            e = l // 2
            z, = _norm_matmul(xf, (norm_mix, l), [(ab_w_in, e)], (ab_b_in, e), F32, tm=1024, tn=1024)
            y = _even_mixer(z.reshape(batch, seq, -1), e, gmlp_w_s, gmlp_b_s_t, gmlp_ln_g, gmlp_ln_b,
                            conv_w, conv_b, conv_ln_g, conv_ln_b, batch, ts=512)
            xf = _proj_residual(y.reshape(t, d), (ab_w_out, e), (ab_b_out, e), xf, tm=512)
        else:
            o = l // 2
            xf = _pool_mixer(xf.reshape(batch, seq, d), (norm_mix, l), o, pool_w, pool_b, pool_scale,
                             batch, ts=512).reshape(t, d)
        q, = _norm_matmul(xf, (norm_xq, l), [(xattn_wq, l)], None, BF16, tm=1024, tn=1024)
        k, v = _norm_matmul(memf, (norm_xkv, l), [(xattn_wk, l), (xattn_wv, l)], None, BF16,
                            tm=batch * N_MEM, tn=512)
        att = _attention(q, k, v, batch, ts=1024)
        xf = _proj_residual(att, (xattn_wo, l), None, xf, tm=512)
        xf = ffn(xf, (norm_ffn2, l), ffn2_gate, ffn2_up, ffn2_down, l)
    return _rmsnorm(xf, norm_final.reshape(1, d), tm=1024).reshape(batch, seq, d)
```

```python
import functools

import jax
import jax.numpy as jnp
from jax import lax
from jax.experimental import pallas as pl
from jax.experimental.pallas import tpu as pltpu

F32 = jnp.float32
BF16 = jnp.bfloat16

DEPTH = 4
CHUNK = 64
N_MEM = 256
N_MEM_HEADS = 4
A_HEADS = 8
GMLP_BLOCK = 128
CONV_WIDTH = 31
POOL_WINDOWS = (2, 4, 8, 16)
EPS = 1e-6

SUBLANES = 8
CONV_HALO = 32
POOL_HALO = 16

VMEM_LIMIT_BYTES = 58 * 1024 * 1024


def _params(*semantics):
    return pltpu.CompilerParams(dimension_semantics=semantics, vmem_limit_bytes=VMEM_LIMIT_BYTES)


def _rms(x, g):
    return x * lax.rsqrt(jnp.mean(x * x, axis=-1, keepdims=True) + EPS) * g


def _layer_norm(x, g, b):
    mu = jnp.mean(x, axis=-1, keepdims=True)
    xc = x - mu
    var = jnp.mean(xc * xc, axis=-1, keepdims=True)
    return xc * lax.rsqrt(var + EPS) * g + b


def _silu(x):
    return x * jax.nn.sigmoid(x)


def _mxu(a, w_ref):
    return jnp.dot(a, w_ref[...].astype(BF16), preferred_element_type=F32)


def _rows_matmul_kernel(*refs, n_w, has_bias, normalized, swiglu):
    refs = list(refs)
    x_ref = refs.pop(0)
    g_ref = None if normalized else refs.pop(0)
    w_refs = [refs.pop(0) for _ in range(n_w)]
    b_ref = refs.pop(0) if has_bias else None
    o_refs = [refs.pop(0) for _ in range(1 if swiglu else n_w)]

    if normalized:
        h = x_ref[...]
    else:
        h_ref, = refs

        @pl.when(pl.program_id(1) == 0)
        def _():
            h_ref[...] = _rms(x_ref[...], g_ref[...]).astype(BF16)

        h = h_ref[...]

    if swiglu:
        gate = _mxu(h, w_refs[0])
        up = _mxu(h, w_refs[1])
        o_refs[0][...] = (_silu(gate) * up).astype(o_refs[0].dtype)
    else:
        for w_ref, o_ref in zip(w_refs, o_refs):
            acc = _mxu(h, w_ref)
            if has_bias:
                acc = acc + b_ref[...]
            o_ref[...] = acc.astype(o_ref.dtype)


def _rows_matmul(x, gain, ws, bias, out_dtype, tm, tn, swiglu=False):
    t, k = x.shape
    n = ws[0][0].shape[-1]
    normalized = gain is None
    in_specs = [pl.BlockSpec((tm, k), lambda i, j: (i, 0))]
    args = [x]
    if not normalized:
        g_arr, gl = gain
        in_specs.append(pl.BlockSpec((None, 1, k), lambda i, j: (gl, 0, 0)))
        args.append(g_arr)
    for w_arr, wl in ws:
        in_specs.append(pl.BlockSpec((None, k, tn), lambda i, j, wl=wl: (wl, 0, j)))
        args.append(w_arr)
    if bias is not None:
        b_arr, bl = bias
        in_specs.append(pl.BlockSpec((None, 1, tn), lambda i, j: (bl, 0, j)))
        args.append(b_arr)
    n_out = 1 if swiglu else len(ws)
    return pl.pallas_call(
        functools.partial(_rows_matmul_kernel, n_w=len(ws), has_bias=bias is not None,
                          normalized=normalized, swiglu=swiglu),
        grid=(t // tm, n // tn),
        in_specs=in_specs,
        out_specs=[pl.BlockSpec((tm, tn), lambda i, j: (i, j)) for _ in range(n_out)],
        out_shape=[jax.ShapeDtypeStruct((t, n), out_dtype) for _ in range(n_out)],
        scratch_shapes=[] if normalized else [pltpu.VMEM((tm, k), BF16)],
        compiler_params=_params("parallel", "arbitrary"),
        name="swiglu" if swiglu else "rows_matmul",
    )(*args)


def _matmul_residual_kernel(a_ref, w_ref, r_ref, o_ref, *, scale):
    o_ref[...] = r_ref[...] + scale * _mxu(a_ref[...], w_ref)


def _matmul_residual(a, w, res, scale, tm, tn):
    t, k = a.shape
    w_arr, wl = w
    n = w_arr.shape[-1]
    return pl.pallas_call(
        functools.partial(_matmul_residual_kernel, scale=scale),
        grid=(t // tm, n // tn),
        in_specs=[
            pl.BlockSpec((tm, k), lambda i, j: (i, 0)),
            pl.BlockSpec((None, k, tn), lambda i, j: (wl, 0, j)),
            pl.BlockSpec((tm, tn), lambda i, j: (i, j)),
        ],
        out_specs=pl.BlockSpec((tm, tn), lambda i, j: (i, j)),
        out_shape=jax.ShapeDtypeStruct((t, n), F32),
        compiler_params=_params("parallel", "arbitrary"),
        name="matmul_residual",
    )(a, w_arr, res)


def _proj_residual_kernel(*refs, has_bias):
    if has_bias:
        a_ref, w_ref, b_ref, r_ref, g_ref, o_ref, hn_ref, wb_ref = refs
    else:
        a_ref, w_ref, r_ref, g_ref, o_ref, hn_ref, wb_ref = refs

    @pl.when(pl.program_id(0) == 0)
    def _():
        wb_ref[...] = w_ref[...].astype(BF16)

    acc = jnp.dot(a_ref[...], wb_ref[...], preferred_element_type=F32)
    if has_bias:
        acc = acc + b_ref[...]
    out = r_ref[...] + acc
    o_ref[...] = out
    hn_ref[...] = _rms(out, g_ref[...]).astype(BF16)


def _proj_residual(a, w, bias, res, next_gain, tm):
    t, k = a.shape
    w_arr, wl = w
    g_arr, gl = next_gain
    n = w_arr.shape[-1]
    in_specs = [
        pl.BlockSpec((tm, k), lambda i: (i, 0)),
        pl.BlockSpec((None, k, n), lambda i: (wl, 0, 0), pipeline_mode=pl.Buffered(1)),
    ]
    args = [a, w_arr]
    if bias is not None:
        b_arr, bl = bias
        in_specs.append(pl.BlockSpec((None, 1, n), lambda i: (bl, 0, 0)))
        args.append(b_arr)
    in_specs += [pl.BlockSpec((tm, n), lambda i: (i, 0)), pl.BlockSpec((None, 1, n), lambda i: (gl, 0, 0))]
    args += [res, g_arr]
    return pl.pallas_call(
        functools.partial(_proj_residual_kernel, has_bias=bias is not None),
        grid=(t // tm,),
        in_specs=in_specs,
        out_specs=[pl.BlockSpec((tm, n), lambda i: (i, 0)), pl.BlockSpec((tm, n), lambda i: (i, 0))],
        out_shape=[jax.ShapeDtypeStruct((t, n), F32), jax.ShapeDtypeStruct((t, n), BF16)],
        scratch_shapes=[pltpu.VMEM((k, n), BF16)],
        compiler_params=_params("arbitrary"),
        name="proj_residual",
    )(*args)


def _attn_kernel(q_ref, k_ref, v_ref, o_ref, *, head_dim):
    scale = head_dim ** -0.5
    for h in range(N_MEM_HEADS):
        sl = slice(h * head_dim, (h + 1) * head_dim)
        s = lax.dot_general(q_ref[:, sl], k_ref[:, sl], (((1,), (1,)), ((), ())),
                            preferred_element_type=F32) * scale
        e = jnp.exp(s - jnp.max(s, axis=-1, keepdims=True))
        p = e / jnp.sum(e, axis=-1, keepdims=True)
        o_ref[:, sl] = jnp.dot(p.astype(BF16), v_ref[:, sl],
                               preferred_element_type=F32).astype(o_ref.dtype)


def _attention(q, k, v, batch, ts):
    t, d = q.shape
    m = k.shape[0] // batch
    tiles = t // batch // ts
    return pl.pallas_call(
        functools.partial(_attn_kernel, head_dim=d // N_MEM_HEADS),
        grid=(batch, tiles),
        in_specs=[
            pl.BlockSpec((ts, d), lambda b, i: (b * tiles + i, 0)),
            pl.BlockSpec((m, d), lambda b, i: (b, 0)),
            pl.BlockSpec((m, d), lambda b, i: (b, 0)),
        ],
        out_specs=pl.BlockSpec((ts, d), lambda b, i: (b * tiles + i, 0)),
        out_shape=jax.ShapeDtypeStruct((t, d), BF16),
        compiler_params=_params("parallel", "arbitrary"),
        name="attention",
    )(q, k, v)


def _even_mixer_kernel(z_ref, zh_ref, ws_ref, bs_ref, glg_ref, glb_ref, cw_ref, cb_ref,
                       clg_ref, clb_ref, y_ref, hh_ref, sh_ref, *, ts, width):
    i = pl.program_id(1)
    hd = width // A_HEADS

    u = jax.nn.gelu(z_ref[:, 0:width])
    v = jax.nn.gelu(z_ref[:, width:2 * width])
    vn = _layer_norm(v, glg_ref[...], glb_ref[...]).astype(BF16)
    row_chunk = lax.broadcasted_iota(jnp.int32, (GMLP_BLOCK, GMLP_BLOCK), 0) // CHUNK
    col_chunk = lax.broadcasted_iota(jnp.int32, (GMLP_BLOCK, GMLP_BLOCK), 1) // CHUNK
    causal = col_chunk <= row_chunk
    nblk = ts // GMLP_BLOCK
    for h in range(A_HEADS):
        cs = slice(h * hd, (h + 1) * hd)
        w = jnp.where(causal, ws_ref[h], 0.0).astype(BF16)
        vh = jnp.concatenate(
            [vn[n * GMLP_BLOCK:(n + 1) * GMLP_BLOCK, cs] for n in range(nblk)], axis=1)
        sp = jnp.dot(w, vh, preferred_element_type=F32) + bs_ref[:, h:h + 1]
        for n in range(nblk):
            rs = slice(n * GMLP_BLOCK, (n + 1) * GMLP_BLOCK)
            y_ref[rs, cs] = (u[rs, cs] * sp[:, n * hd:(n + 1) * hd]).astype(y_ref.dtype)

    hh_ref[CONV_HALO:, :] = z_ref[:, 2 * width:3 * width] * jax.nn.sigmoid(z_ref[:, 3 * width:4 * width])
    halo = zh_ref[:, 0:width] * jax.nn.sigmoid(zh_ref[:, width:2 * width])
    hh_ref[0:CONV_HALO, :] = jnp.where(i == 0, 0.0, halo)
    first = CONV_HALO - (CONV_WIDTH - 1)
    span = ts + CONV_HALO - SUBLANES
    acc = None
    for res in range(SUBLANES):
        offsets = [j for j in range(first, first + CONV_WIDTH) if j % SUBLANES == res]
        if res != 0:
            sh_ref[res - 1] = hh_ref[pl.ds(res, span), :]
        for j in offsets:
            if res == 0:
                tap = hh_ref[pl.ds(j, ts), :]
            else:
                tap = sh_ref[res - 1, pl.ds(j - res, ts), :]
            term = cw_ref[j - first:j - first + 1, :] * tap
            acc = term if acc is None else acc + term
    acc = acc + cb_ref[...]
    y_ref[:, width:2 * width] = _silu(_layer_norm(acc, clg_ref[...], clb_ref[...])).astype(y_ref.dtype)


def _even_mixer(z, e, w_s, b_s_t, gln_g, gln_b, conv_w, conv_b, cln_g, cln_b, batch, ts):
    _, s, zw = z.shape
    width = zw // 4
    tiles = s // ts
    halo_per_tile = ts // CONV_HALO
    vec = lambda: pl.BlockSpec((None, 1, width), lambda b, i: (e, 0, 0))
    return pl.pallas_call(
        functools.partial(_even_mixer_kernel, ts=ts, width=width),
        grid=(batch, tiles),
        in_specs=[
            pl.BlockSpec((None, ts, zw), lambda b, i: (b, i, 0)),
            pl.BlockSpec((None, CONV_HALO, 2 * width),
                         lambda b, i: (b, jnp.maximum(i * halo_per_tile - 1, 0), 1)),
            pl.BlockSpec((None, A_HEADS, GMLP_BLOCK, GMLP_BLOCK), lambda b, i: (e, 0, 0, 0)),
            pl.BlockSpec((None, GMLP_BLOCK, A_HEADS), lambda b, i: (e, 0, 0)),
            vec(), vec(),
            pl.BlockSpec((None, CONV_WIDTH, width), lambda b, i: (e, 0, 0)),
            vec(), vec(), vec(),
        ],
        out_specs=pl.BlockSpec((None, ts, 2 * width), lambda b, i: (b, i, 0)),
        out_shape=jax.ShapeDtypeStruct((batch, s, 2 * width), BF16),
        scratch_shapes=[
            pltpu.VMEM((ts + CONV_HALO, width), F32),
            pltpu.VMEM((SUBLANES - 1, ts + CONV_HALO - SUBLANES, width), F32),
        ],
        compiler_params=_params("parallel", "arbitrary"),
        name="even_mixer",
    )(z, z, w_s, b_s_t, gln_g, gln_b, conv_w, conv_b, cln_g, cln_b)


def _pool_mixer_kernel(x_ref, xh_ref, g_ref, w_ref, b_ref, sc_ref, gn_ref, o_ref, hn_ref, *, ts, group_dim):
    i = pl.program_id(1)
    g = g_ref[...]
    x = x_ref[...]
    h = _rms(x, g)
    h_halo = jnp.where(i == 0, 0.0, _rms(xh_ref[...], g))
    hist = jnp.concatenate([h_halo, h], axis=0)
    frames = i * ts + lax.broadcasted_iota(jnp.int32, (ts, 1), 0) + 1
    for gi, win in enumerate(POOL_WINDOWS):
        cs = slice(gi * group_dim, (gi + 1) * group_dim)
        wsum = hist[:, cs]
        span = 1
        while span < win:
            wsum = wsum + pltpu.roll(wsum, span, axis=0)
            span *= 2
        mean = wsum[POOL_HALO:, :] / jnp.minimum(frames, win).astype(F32)
        d = (mean - h[:, cs]).astype(BF16)
        out = jnp.dot(d, w_ref[gi].astype(BF16), preferred_element_type=F32) + b_ref[:, cs]
        o_ref[:, cs] = x[:, cs] + out * sc_ref[:, cs]
    hn_ref[...] = _rms(o_ref[...], gn_ref[...]).astype(BF16)


def _pool_mixer(x, gain, o, pool_w, pool_b, pool_scale, next_gain, batch, ts):
    _, s, d = x.shape
    g_arr, gl = gain
    gn_arr, gnl = next_gain
    groups = len(POOL_WINDOWS)
    group_dim = d // groups
    tiles = s // ts
    halo_per_tile = ts // POOL_HALO
    tile = lambda: pl.BlockSpec((None, ts, d), lambda b, i: (b, i, 0))
    return pl.pallas_call(
        functools.partial(_pool_mixer_kernel, ts=ts, group_dim=group_dim),
        grid=(batch, tiles),
        in_specs=[
            tile(),
            pl.BlockSpec((None, POOL_HALO, d), lambda b, i: (b, jnp.maximum(i * halo_per_tile - 1, 0), 0)),
            pl.BlockSpec((None, 1, d), lambda b, i: (gl, 0, 0)),
            pl.BlockSpec((None, groups, group_dim, group_dim), lambda b, i: (o, 0, 0, 0)),
            pl.BlockSpec((None, 1, d), lambda b, i: (o, 0, 0)),
            pl.BlockSpec((None, 1, d), lambda b, i: (o, 0, 0)),
            pl.BlockSpec((None, 1, d), lambda b, i: (gnl, 0, 0)),
        ],
        out_specs=[tile(), tile()],
        out_shape=[jax.ShapeDtypeStruct(x.shape, F32), jax.ShapeDtypeStruct(x.shape, BF16)],
        compiler_params=_params("parallel", "arbitrary"),
        name="pool_mixer",
    )(x, x, g_arr, pool_w, pool_b, pool_scale, gn_arr)


def _rmsnorm_kernel(x_ref, g_ref, o_ref):
    o_ref[...] = _rms(x_ref[...], g_ref[...])


def _rmsnorm(x, g, tm):
    t, d = x.shape
    return pl.pallas_call(
        _rmsnorm_kernel,
        grid=(t // tm,),
        in_specs=[pl.BlockSpec((tm, d), lambda i: (i, 0)), pl.BlockSpec((1, d), lambda i: (0, 0))],
        out_specs=pl.BlockSpec((tm, d), lambda i: (i, 0)),
        out_shape=jax.ShapeDtypeStruct((t, d), F32),
        compiler_params=_params("parallel"),
        name="final_rmsnorm",
    )(x, g)


def kernel(x, mem, norm_ffn1, ffn1_gate, ffn1_up, ffn1_down, norm_mix, ab_w_in, ab_b_in, gmlp_w_s, gmlp_b_s, gmlp_ln_g, gmlp_ln_b, conv_w, conv_b, conv_ln_g, conv_ln_b, ab_w_out, ab_b_out, pool_w, pool_b, pool_scale, norm_xq, norm_xkv, xattn_wq, xattn_wk, xattn_wv, xattn_wo, norm_ffn2, ffn2_gate, ffn2_up, ffn2_down, norm_final):
    batch, seq, d = x.shape
    t = batch * seq
    row = lambda a: a.reshape(a.shape[0], 1, a.shape[-1])

    norm_ffn1, norm_mix, norm_xq, norm_xkv, norm_ffn2 = map(row, (norm_ffn1, norm_mix, norm_xq, norm_xkv, norm_ffn2))
    ab_b_in, ab_b_out, pool_scale = row(ab_b_in), row(ab_b_out), row(pool_scale)
    pool_b = pool_b.reshape(pool_b.shape[0], 1, d)
    gmlp_b_s_t = jnp.swapaxes(gmlp_b_s, 1, 2)
    gmlp_ln_g, gmlp_ln_b, conv_b, conv_ln_g, conv_ln_b = map(row, (gmlp_ln_g, gmlp_ln_b, conv_b, conv_ln_g, conv_ln_b))
    conv_w = conv_w.reshape(conv_w.shape[0], CONV_WIDTH, conv_w.shape[-1])
    ffn1_down, ffn2_down, ab_w_in = (w.astype(BF16) for w in (ffn1_down, ffn2_down, ab_w_in))

    xf = x.reshape(t, d)
    memf = mem.reshape(batch * N_MEM, d)

    for l in range(DEPTH):
        hidden, = _rows_matmul(xf, (norm_ffn1, l), [(ffn1_gate, l), (ffn1_up, l)], None, BF16,
                               tm=1024, tn=512, swiglu=True)
        xf = _matmul_residual(hidden, (ffn1_down, l), xf, 0.5, tm=1024, tn=512)
        if l % 2 == 0:
            e = l // 2
            z, = _rows_matmul(xf, (norm_mix, l), [(ab_w_in, e)], (ab_b_in, e), F32, tm=1024, tn=1024)
            y = _even_mixer(z.reshape(batch, seq, -1), e, gmlp_w_s, gmlp_b_s_t, gmlp_ln_g, gmlp_ln_b,
                            conv_w, conv_b, conv_ln_g, conv_ln_b, batch, ts=512)
            xf, hq = _proj_residual(y.reshape(t, d), (ab_w_out, e), (ab_b_out, e), xf, (norm_xq, l), tm=512)
        else:
            o = l // 2
            xf, hq = _pool_mixer(xf.reshape(batch, seq, d), (norm_mix, l), o, pool_w, pool_b, pool_scale,
                                 (norm_xq, l), batch, ts=512)
            xf, hq = xf.reshape(t, d), hq.reshape(t, d)
        q, = _rows_matmul(hq, None, [(xattn_wq, l)], None, BF16, tm=2048, tn=1024)
        k, v = _rows_matmul(memf, (norm_xkv, l), [(xattn_wk, l), (xattn_wv, l)], None, BF16,
                            tm=batch * N_MEM, tn=512)
        att = _attention(q, k, v, batch, ts=1024)
        xf, hf = _proj_residual(att, (xattn_wo, l), None, xf, (norm_ffn2, l), tm=512)
        hidden, = _rows_matmul(hf, None, [(ffn2_gate, l), (ffn2_up, l)], None, BF16,
                               tm=2048, tn=512, swiglu=True)
        xf = _matmul_residual(hidden, (ffn2_down, l), xf, 0.5, tm=1024, tn=512)
    return _rmsnorm(xf, norm_final.reshape(1, d), tm=1024).reshape(batch, seq, d)
```

```python
import functools

import jax
import jax.numpy as jnp
from jax import lax
from jax.experimental import pallas as pl
from jax.experimental.pallas import tpu as pltpu

F32 = jnp.float32
BF16 = jnp.bfloat16

DEPTH = 4
CHUNK = 64
N_MEM = 256
N_MEM_HEADS = 4
A_HEADS = 8
GMLP_BLOCK = 128
CONV_WIDTH = 31
POOL_WINDOWS = (2, 4, 8, 16)
EPS = 1e-6

SUBLANES = 8
MXU_COLS = 256
MATMUL_ROW_CHUNK = 1024
ATTN_ROW_CHUNK = 512
CONV_HALO = 32
POOL_HALO = 16

VMEM_LIMIT_BYTES = 58 * 1024 * 1024


def _params(*semantics):
    return pltpu.CompilerParams(dimension_semantics=semantics, vmem_limit_bytes=VMEM_LIMIT_BYTES)


def _rms(x, g):
    return x * lax.rsqrt(jnp.mean(x * x, axis=-1, keepdims=True) + EPS) * g


def _layer_norm(x, g, b):
    mu = jnp.mean(x, axis=-1, keepdims=True)
    xc = x - mu
    var = jnp.mean(xc * xc, axis=-1, keepdims=True)
    return xc * lax.rsqrt(var + EPS) * g + b


def _silu(x):
    return x * jax.nn.sigmoid(x)


def _mxu(a, w_ref):
    return jnp.dot(a, w_ref[...].astype(BF16), preferred_element_type=F32)


def _rows_matmul_kernel(*refs, n_w, has_bias, normalized, swiglu, row_chunk):
    refs = list(refs)
    x_ref = refs.pop(0)
    g_ref = None if normalized else refs.pop(0)
    w_refs = [refs.pop(0) for _ in range(n_w)]
    b_ref = refs.pop(0) if has_bias else None
    o_refs = [refs.pop(0) for _ in range(1 if swiglu else n_w)]

    if normalized:
        h_ref = x_ref
    else:
        h_ref, = refs

        @pl.when(pl.program_id(1) == 0)
        def _():
            h_ref[...] = _rms(x_ref[...], g_ref[...]).astype(BF16)

    tm, tn = o_refs[0].shape
    for c in range(tn // MXU_COLS):
        cs = slice(c * MXU_COLS, (c + 1) * MXU_COLS)
        w_cols = [w_ref[:, cs].astype(BF16) for w_ref in w_refs]
        for r in range(tm // row_chunk):
            rs = slice(r * row_chunk, (r + 1) * row_chunk)
            accs = [jnp.dot(h_ref[rs, :], w, preferred_element_type=F32) for w in w_cols]
            if swiglu:
                o_refs[0][rs, cs] = (_silu(accs[0]) * accs[1]).astype(o_refs[0].dtype)
            else:
                for acc, o_ref in zip(accs, o_refs):
                    if has_bias:
                        acc = acc + b_ref[:, cs]
                    o_ref[rs, cs] = acc.astype(o_ref.dtype)


def _rows_matmul(x, gain, ws, bias, out_dtype, tm, tn, swiglu=False):
    t, k = x.shape
    n = ws[0][0].shape[-1]
    normalized = gain is None
    in_specs = [pl.BlockSpec((tm, k), lambda i, j: (i, 0))]
    args = [x]
    if not normalized:
        g_arr, gl = gain
        in_specs.append(pl.BlockSpec((None, 1, k), lambda i, j: (gl, 0, 0)))
        args.append(g_arr)
    for w_arr, wl in ws:
        in_specs.append(pl.BlockSpec((None, k, tn), lambda i, j, wl=wl: (wl, 0, j)))
        args.append(w_arr)
    if bias is not None:
        b_arr, bl = bias
        in_specs.append(pl.BlockSpec((None, 1, tn), lambda i, j: (bl, 0, j)))
        args.append(b_arr)
    n_out = 1 if swiglu else len(ws)
    return pl.pallas_call(
        functools.partial(_rows_matmul_kernel, n_w=len(ws), has_bias=bias is not None,
                          normalized=normalized, swiglu=swiglu, row_chunk=min(tm, MATMUL_ROW_CHUNK)),
        grid=(t // tm, n // tn),
        in_specs=in_specs,
        out_specs=[pl.BlockSpec((tm, tn), lambda i, j: (i, j)) for _ in range(n_out)],
        out_shape=[jax.ShapeDtypeStruct((t, n), out_dtype) for _ in range(n_out)],
        scratch_shapes=[] if normalized else [pltpu.VMEM((tm, k), BF16)],
        compiler_params=_params("parallel", "arbitrary"),
        name="swiglu" if swiglu else "rows_matmul",
    )(*args)


def _memory_kv(mem, gain, wk, wv, tn):
    m, k = mem.shape
    layers, _, n = wk.shape
    w_spec = lambda: pl.BlockSpec((None, k, tn), lambda l, j: (l, 0, j))
    o_spec = lambda: pl.BlockSpec((None, m, tn), lambda l, j: (l, 0, j))
    return pl.pallas_call(
        functools.partial(_rows_matmul_kernel, n_w=2, has_bias=False, normalized=False, swiglu=False,
                          row_chunk=min(m, MATMUL_ROW_CHUNK)),
        grid=(layers, n // tn),
        in_specs=[
            pl.BlockSpec((m, k), lambda l, j: (0, 0)),
            pl.BlockSpec((None, 1, k), lambda l, j: (l, 0, 0)),
            w_spec(), w_spec(),
        ],
        out_specs=[o_spec(), o_spec()],
        out_shape=[jax.ShapeDtypeStruct((layers, m, n), BF16)] * 2,
        scratch_shapes=[pltpu.VMEM((m, k), BF16)],
        compiler_params=_params("parallel", "arbitrary"),
        name="memory_kv",
    )(mem, gain, wk, wv)


def _matmul_residual_kernel(a_ref, w_ref, r_ref, o_ref, *, scale):
    o_ref[...] = r_ref[...] + scale * _mxu(a_ref[...], w_ref)


def _matmul_residual(a, w, res, scale, tm, tn):
    t, k = a.shape
    w_arr, wl = w
    n = w_arr.shape[-1]
    return pl.pallas_call(
        functools.partial(_matmul_residual_kernel, scale=scale),
        grid=(t // tm, n // tn),
        in_specs=[
            pl.BlockSpec((tm, k), lambda i, j: (i, 0)),
            pl.BlockSpec((None, k, tn), lambda i, j: (wl, 0, j)),
            pl.BlockSpec((tm, tn), lambda i, j: (i, j)),
        ],
        out_specs=pl.BlockSpec((tm, tn), lambda i, j: (i, j)),
        out_shape=jax.ShapeDtypeStruct((t, n), F32),
        compiler_params=_params("parallel", "arbitrary"),
        name="matmul_residual",
    )(a, w_arr, res)


def _proj_residual_kernel(*refs, has_bias):
    if has_bias:
        a_ref, w_ref, b_ref, r_ref, g_ref, o_ref, hn_ref, wb_ref = refs
    else:
        a_ref, w_ref, r_ref, g_ref, o_ref, hn_ref, wb_ref = refs

    @pl.when(pl.program_id(0) == 0)
    def _():
        wb_ref[...] = w_ref[...].astype(BF16)

    acc = jnp.dot(a_ref[...], wb_ref[...], preferred_element_type=F32)
    if has_bias:
        acc = acc + b_ref[...]
    out = r_ref[...] + acc
    o_ref[...] = out
    hn_ref[...] = _rms(out, g_ref[...]).astype(BF16)


def _proj_residual(a, w, bias, res, next_gain, tm):
    t, k = a.shape
    w_arr, wl = w
    g_arr, gl = next_gain
    n = w_arr.shape[-1]
    in_specs = [
        pl.BlockSpec((tm, k), lambda i: (i, 0)),
        pl.BlockSpec((None, k, n), lambda i: (wl, 0, 0), pipeline_mode=pl.Buffered(1)),
    ]
    args = [a, w_arr]
    if bias is not None:
        b_arr, bl = bias
        in_specs.append(pl.BlockSpec((None, 1, n), lambda i: (bl, 0, 0)))
        args.append(b_arr)
    in_specs += [pl.BlockSpec((tm, n), lambda i: (i, 0)), pl.BlockSpec((None, 1, n), lambda i: (gl, 0, 0))]
    args += [res, g_arr]
    return pl.pallas_call(
        functools.partial(_proj_residual_kernel, has_bias=bias is not None),
        grid=(t // tm,),
        in_specs=in_specs,
        out_specs=[pl.BlockSpec((tm, n), lambda i: (i, 0)), pl.BlockSpec((tm, n), lambda i: (i, 0))],
        out_shape=[jax.ShapeDtypeStruct((t, n), F32), jax.ShapeDtypeStruct((t, n), BF16)],
        scratch_shapes=[pltpu.VMEM((k, n), BF16)],
        compiler_params=_params("arbitrary"),
        name="proj_residual",
    )(*args)


def _attn_kernel(q_ref, k_ref, v_ref, o_ref, *, head_dim):
    scale = head_dim ** -0.5
    for r in range(q_ref.shape[0] // ATTN_ROW_CHUNK):
        rs = slice(r * ATTN_ROW_CHUNK, (r + 1) * ATTN_ROW_CHUNK)
        for h in range(N_MEM_HEADS):
            sl = slice(h * head_dim, (h + 1) * head_dim)
            s = lax.dot_general(q_ref[rs, sl], k_ref[:, sl], (((1,), (1,)), ((), ())),
                                preferred_element_type=F32) * scale
            e = jnp.exp(s - jnp.max(s, axis=-1, keepdims=True))
            p = e / jnp.sum(e, axis=-1, keepdims=True)
            o_ref[rs, sl] = jnp.dot(p.astype(BF16), v_ref[:, sl],
                                    preferred_element_type=F32).astype(o_ref.dtype)


def _attention(q, k, v, layer, batch, ts):
    t, d = q.shape
    m = k.shape[1] // batch
    tiles = t // batch // ts
    return pl.pallas_call(
        functools.partial(_attn_kernel, head_dim=d // N_MEM_HEADS),
        grid=(batch, tiles),
        in_specs=[
            pl.BlockSpec((ts, d), lambda b, i: (b * tiles + i, 0)),
            pl.BlockSpec((None, m, d), lambda b, i: (layer, b, 0)),
            pl.BlockSpec((None, m, d), lambda b, i: (layer, b, 0)),
        ],
        out_specs=pl.BlockSpec((ts, d), lambda b, i: (b * tiles + i, 0)),
        out_shape=jax.ShapeDtypeStruct((t, d), BF16),
        compiler_params=_params("parallel", "arbitrary"),
        name="attention",
    )(q, k, v)


def _even_mixer_kernel(z_ref, zh_ref, ws_ref, bs_ref, glg_ref, glb_ref, cw_ref, cb_ref,
                       clg_ref, clb_ref, y_ref, hh_ref, sh_ref, *, ts, width):
    i = pl.program_id(1)
    hd = width // A_HEADS

    u = jax.nn.gelu(z_ref[:, 0:width])
    v = jax.nn.gelu(z_ref[:, width:2 * width])
    vn = _layer_norm(v, glg_ref[...], glb_ref[...]).astype(BF16)
    row_chunk = lax.broadcasted_iota(jnp.int32, (GMLP_BLOCK, GMLP_BLOCK), 0) // CHUNK
    col_chunk = lax.broadcasted_iota(jnp.int32, (GMLP_BLOCK, GMLP_BLOCK), 1) // CHUNK
    causal = col_chunk <= row_chunk
    nblk = ts // GMLP_BLOCK
    for h in range(A_HEADS):
        cs = slice(h * hd, (h + 1) * hd)
        w = jnp.where(causal, ws_ref[h], 0.0).astype(BF16)
        vh = jnp.concatenate(
            [vn[n * GMLP_BLOCK:(n + 1) * GMLP_BLOCK, cs] for n in range(nblk)], axis=1)
        sp = jnp.dot(w, vh, preferred_element_type=F32) + bs_ref[:, h:h + 1]
        for n in range(nblk):
            rs = slice(n * GMLP_BLOCK, (n + 1) * GMLP_BLOCK)
            y_ref[rs, cs] = (u[rs, cs] * sp[:, n * hd:(n + 1) * hd]).astype(y_ref.dtype)

    hh_ref[CONV_HALO:, :] = z_ref[:, 2 * width:3 * width] * jax.nn.sigmoid(z_ref[:, 3 * width:4 * width])
    halo = zh_ref[:, 0:width] * jax.nn.sigmoid(zh_ref[:, width:2 * width])
    hh_ref[0:CONV_HALO, :] = jnp.where(i == 0, 0.0, halo)
    first = CONV_HALO - (CONV_WIDTH - 1)
    span = ts + CONV_HALO - SUBLANES
    acc = None
    for res in range(SUBLANES):
        offsets = [j for j in range(first, first + CONV_WIDTH) if j % SUBLANES == res]
        if res != 0:
            sh_ref[res - 1] = hh_ref[pl.ds(res, span), :]
        for j in offsets:
            if res == 0:
                tap = hh_ref[pl.ds(j, ts), :]
            else:
                tap = sh_ref[res - 1, pl.ds(j - res, ts), :]
            term = cw_ref[j - first:j - first + 1, :] * tap
            acc = term if acc is None else acc + term
    acc = acc + cb_ref[...]
    y_ref[:, width:2 * width] = _silu(_layer_norm(acc, clg_ref[...], clb_ref[...])).astype(y_ref.dtype)


def _even_mixer(z, e, w_s, b_s_t, gln_g, gln_b, conv_w, conv_b, cln_g, cln_b, batch, ts):
    _, s, zw = z.shape
    width = zw // 4
    tiles = s // ts
    halo_per_tile = ts // CONV_HALO
    vec = lambda: pl.BlockSpec((None, 1, width), lambda b, i: (e, 0, 0))
    return pl.pallas_call(
        functools.partial(_even_mixer_kernel, ts=ts, width=width),
        grid=(batch, tiles),
        in_specs=[
            pl.BlockSpec((None, ts, zw), lambda b, i: (b, i, 0)),
            pl.BlockSpec((None, CONV_HALO, 2 * width),
                         lambda b, i: (b, jnp.maximum(i * halo_per_tile - 1, 0), 1)),
            pl.BlockSpec((None, A_HEADS, GMLP_BLOCK, GMLP_BLOCK), lambda b, i: (e, 0, 0, 0)),
            pl.BlockSpec((None, GMLP_BLOCK, A_HEADS), lambda b, i: (e, 0, 0)),
            vec(), vec(),
            pl.BlockSpec((None, CONV_WIDTH, width), lambda b, i: (e, 0, 0)),
            vec(), vec(), vec(),
        ],
        out_specs=pl.BlockSpec((None, ts, 2 * width), lambda b, i: (b, i, 0)),
        out_shape=jax.ShapeDtypeStruct((batch, s, 2 * width), BF16),
        scratch_shapes=[
            pltpu.VMEM((ts + CONV_HALO, width), F32),
            pltpu.VMEM((SUBLANES - 1, ts + CONV_HALO - SUBLANES, width), F32),
        ],
        compiler_params=_params("parallel", "arbitrary"),
        name="even_mixer",
    )(z, z, w_s, b_s_t, gln_g, gln_b, conv_w, conv_b, cln_g, cln_b)


def _pool_mixer_kernel(x_ref, xh_ref, g_ref, w_ref, b_ref, sc_ref, gn_ref, o_ref, hn_ref, *, ts, group_dim):
    i = pl.program_id(1)
    g = g_ref[...]
    x = x_ref[...]
    h = _rms(x, g)
    h_halo = jnp.where(i == 0, 0.0, _rms(xh_ref[...], g))
    hist = jnp.concatenate([h_halo, h], axis=0)
    frames = i * ts + lax.broadcasted_iota(jnp.int32, (ts, 1), 0) + 1
    for gi, win in enumerate(POOL_WINDOWS):
        cs = slice(gi * group_dim, (gi + 1) * group_dim)
        wsum = hist[:, cs]
        span = 1
        while span < win:
            wsum = wsum + pltpu.roll(wsum, span, axis=0)
            span *= 2
        mean = wsum[POOL_HALO:, :] / jnp.minimum(frames, win).astype(F32)
        d = (mean - h[:, cs]).astype(BF16)
        out = jnp.dot(d, w_ref[gi].astype(BF16), preferred_element_type=F32) + b_ref[:, cs]
        o_ref[:, cs] = x[:, cs] + out * sc_ref[:, cs]
    hn_ref[...] = _rms(o_ref[...], gn_ref[...]).astype(BF16)


def _pool_mixer(x, gain, o, pool_w, pool_b, pool_scale, next_gain, batch, ts):
    _, s, d = x.shape
    g_arr, gl = gain
    gn_arr, gnl = next_gain
    groups = len(POOL_WINDOWS)
    group_dim = d // groups
    tiles = s // ts
    halo_per_tile = ts // POOL_HALO
    tile = lambda: pl.BlockSpec((None, ts, d), lambda b, i: (b, i, 0))
    return pl.pallas_call(
        functools.partial(_pool_mixer_kernel, ts=ts, group_dim=group_dim),
        grid=(batch, tiles),
        in_specs=[
            tile(),
            pl.BlockSpec((None, POOL_HALO, d), lambda b, i: (b, jnp.maximum(i * halo_per_tile - 1, 0), 0)),
            pl.BlockSpec((None, 1, d), lambda b, i: (gl, 0, 0)),
            pl.BlockSpec((None, groups, group_dim, group_dim), lambda b, i: (o, 0, 0, 0)),
            pl.BlockSpec((None, 1, d), lambda b, i: (o, 0, 0)),
            pl.BlockSpec((None, 1, d), lambda b, i: (o, 0, 0)),
            pl.BlockSpec((None, 1, d), lambda b, i: (gnl, 0, 0)),
        ],
        out_specs=[tile(), tile()],
        out_shape=[jax.ShapeDtypeStruct(x.shape, F32), jax.ShapeDtypeStruct(x.shape, BF16)],
        compiler_params=_params("parallel", "arbitrary"),
        name="pool_mixer",
    )(x, x, g_arr, pool_w, pool_b, pool_scale, gn_arr)


def _rmsnorm_kernel(x_ref, g_ref, o_ref):
    o_ref[...] = _rms(x_ref[...], g_ref[...])


def _rmsnorm(x, g, tm):
    t, d = x.shape
    return pl.pallas_call(
        _rmsnorm_kernel,
        grid=(t // tm,),
        in_specs=[pl.BlockSpec((tm, d), lambda i: (i, 0)), pl.BlockSpec((1, d), lambda i: (0, 0))],
        out_specs=pl.BlockSpec((tm, d), lambda i: (i, 0)),
        out_shape=jax.ShapeDtypeStruct((t, d), F32),
        compiler_params=_params("parallel"),
        name="final_rmsnorm",
    )(x, g)


def kernel(x, mem, norm_ffn1, ffn1_gate, ffn1_up, ffn1_down, norm_mix, ab_w_in, ab_b_in, gmlp_w_s, gmlp_b_s, gmlp_ln_g, gmlp_ln_b, conv_w, conv_b, conv_ln_g, conv_ln_b, ab_w_out, ab_b_out, pool_w, pool_b, pool_scale, norm_xq, norm_xkv, xattn_wq, xattn_wk, xattn_wv, xattn_wo, norm_ffn2, ffn2_gate, ffn2_up, ffn2_down, norm_final):
    batch, seq, d = x.shape
    t = batch * seq
    row = lambda a: a.reshape(a.shape[0], 1, a.shape[-1])

    norm_ffn1, norm_mix, norm_xq, norm_xkv, norm_ffn2 = map(row, (norm_ffn1, norm_mix, norm_xq, norm_xkv, norm_ffn2))
    ab_b_in, ab_b_out, pool_scale = row(ab_b_in), row(ab_b_out), row(pool_scale)
    pool_b = pool_b.reshape(pool_b.shape[0], 1, d)
    gmlp_b_s_t = jnp.swapaxes(gmlp_b_s, 1, 2)
    gmlp_ln_g, gmlp_ln_b, conv_b, conv_ln_g, conv_ln_b = map(row, (gmlp_ln_g, gmlp_ln_b, conv_b, conv_ln_g, conv_ln_b))
    conv_w = conv_w.reshape(conv_w.shape[0], CONV_WIDTH, conv_w.shape[-1])
    ffn1_down, ffn2_down, ab_w_in = (w.astype(BF16) for w in (ffn1_down, ffn2_down, ab_w_in))

    xf = x.reshape(t, d)
    k_all, v_all = _memory_kv(mem.reshape(batch * N_MEM, d), norm_xkv, xattn_wk, xattn_wv, tn=512)

    for l in range(DEPTH):
        hidden, = _rows_matmul(xf, (norm_ffn1, l), [(ffn1_gate, l), (ffn1_up, l)], None, BF16,
                               tm=1024, tn=512, swiglu=True)
        xf = _matmul_residual(hidden, (ffn1_down, l), xf, 0.5, tm=1024, tn=512)
        if l % 2 == 0:
            e = l // 2
            z, = _rows_matmul(xf, (norm_mix, l), [(ab_w_in, e)], (ab_b_in, e), F32, tm=1024, tn=1024)
            y = _even_mixer(z.reshape(batch, seq, -1), e, gmlp_w_s, gmlp_b_s_t, gmlp_ln_g, gmlp_ln_b,
                            conv_w, conv_b, conv_ln_g, conv_ln_b, batch, ts=512)
            xf, hq = _proj_residual(y.reshape(t, d), (ab_w_out, e), (ab_b_out, e), xf, (norm_xq, l), tm=512)
        else:
            o = l // 2
            xf, hq = _pool_mixer(xf.reshape(batch, seq, d), (norm_mix, l), o, pool_w, pool_b, pool_scale,
                                 (norm_xq, l), batch, ts=512)
            xf, hq = xf.reshape(t, d), hq.reshape(t, d)
        q, = _rows_matmul(hq, None, [(xattn_wq, l)], None, BF16, tm=2048, tn=1024)
        att = _attention(q, k_all, v_all, l, batch, ts=seq)
        xf, hf = _proj_residual(att, (xattn_wo, l), None, xf, (norm_ffn2, l), tm=512)
        hidden, = _rows_matmul(hf, None, [(ffn2_gate, l), (ffn2_up, l)], None, BF16,
                               tm=2048, tn=512, swiglu=True)
        xf = _matmul_residual(hidden, (ffn2_down, l), xf, 0.5, tm=1024, tn=512)
    return _rmsnorm(xf, norm_final.reshape(1, d), tm=1024).reshape(batch, seq, d)
```

```python
import functools

import jax
import jax.numpy as jnp
from jax import lax
from jax.experimental import pallas as pl
from jax.experimental.pallas import tpu as pltpu

F32 = jnp.float32
BF16 = jnp.bfloat16

DEPTH = 4
CHUNK = 64
N_MEM = 256
N_MEM_HEADS = 4
A_HEADS = 8
GMLP_BLOCK = 128
CONV_WIDTH = 31
POOL_WINDOWS = (2, 4, 8, 16)
EPS = 1e-6

SUBLANES = 8
MXU_COLS = 256
MATMUL_ROW_CHUNK = 1024
ATTN_ROW_CHUNK = 512
FFN_DOWN_COLS = 512
CONV_HALO = 32
POOL_HALO = 16

VMEM_LIMIT_BYTES = 58 * 1024 * 1024


def _params(*semantics):
    return pltpu.CompilerParams(dimension_semantics=semantics, vmem_limit_bytes=VMEM_LIMIT_BYTES)


def _rms(x, g):
    return x * lax.rsqrt(jnp.mean(x * x, axis=-1, keepdims=True) + EPS) * g


def _layer_norm(x, g, b):
    mu = jnp.mean(x, axis=-1, keepdims=True)
    xc = x - mu
    var = jnp.mean(xc * xc, axis=-1, keepdims=True)
    return xc * lax.rsqrt(var + EPS) * g + b


def _silu(x):
    return x * jax.nn.sigmoid(x)


def _rows_matmul_kernel(*refs, n_w, has_bias, normalized, swiglu, row_chunk):
    refs = list(refs)
    x_ref = refs.pop(0)
    g_ref = None if normalized else refs.pop(0)
    w_refs = [refs.pop(0) for _ in range(n_w)]
    b_ref = refs.pop(0) if has_bias else None
    o_refs = [refs.pop(0) for _ in range(1 if swiglu else n_w)]

    if normalized:
        h_ref = x_ref
    else:
        h_ref, = refs

        @pl.when(pl.program_id(1) == 0)
        def _():
            h_ref[...] = _rms(x_ref[...], g_ref[...]).astype(BF16)

    tm, tn = o_refs[0].shape
    for c in range(tn // MXU_COLS):
        cs = slice(c * MXU_COLS, (c + 1) * MXU_COLS)
        w_cols = [w_ref[:, cs].astype(BF16) for w_ref in w_refs]
        for r in range(tm // row_chunk):
            rs = slice(r * row_chunk, (r + 1) * row_chunk)
            accs = [jnp.dot(h_ref[rs, :], w, preferred_element_type=F32) for w in w_cols]
            if swiglu:
                o_refs[0][rs, cs] = (_silu(accs[0]) * accs[1]).astype(o_refs[0].dtype)
            else:
                for acc, o_ref in zip(accs, o_refs):
                    if has_bias:
                        acc = acc + b_ref[:, cs]
                    o_ref[rs, cs] = acc.astype(o_ref.dtype)


def _rows_matmul(x, gain, ws, bias, out_dtype, tm, tn, swiglu=False):
    t, k = x.shape
    n = ws[0][0].shape[-1]
    normalized = gain is None
    in_specs = [pl.BlockSpec((tm, k), lambda i, j: (i, 0))]
    args = [x]
    if not normalized:
        g_arr, gl = gain
        in_specs.append(pl.BlockSpec((None, 1, k), lambda i, j: (gl, 0, 0)))
        args.append(g_arr)
    for w_arr, wl in ws:
        in_specs.append(pl.BlockSpec((None, k, tn), lambda i, j, wl=wl: (wl, 0, j)))
        args.append(w_arr)
    if bias is not None:
        b_arr, bl = bias
        in_specs.append(pl.BlockSpec((None, 1, tn), lambda i, j: (bl, 0, j)))
        args.append(b_arr)
    n_out = 1 if swiglu else len(ws)
    return pl.pallas_call(
        functools.partial(_rows_matmul_kernel, n_w=len(ws), has_bias=bias is not None,
                          normalized=normalized, swiglu=swiglu, row_chunk=min(tm, MATMUL_ROW_CHUNK)),
        grid=(t // tm, n // tn),
        in_specs=in_specs,
        out_specs=[pl.BlockSpec((tm, tn), lambda i, j: (i, j)) for _ in range(n_out)],
        out_shape=[jax.ShapeDtypeStruct((t, n), out_dtype) for _ in range(n_out)],
        scratch_shapes=[] if normalized else [pltpu.VMEM((tm, k), BF16)],
        compiler_params=_params("parallel", "arbitrary"),
        name="swiglu" if swiglu else "rows_matmul",
    )(*args)


def _memory_kv(mem, gain, wk, wv, tn):
    m, k = mem.shape
    layers, _, n = wk.shape
    w_spec = lambda: pl.BlockSpec((None, k, tn), lambda l, j: (l, 0, j))
    o_spec = lambda: pl.BlockSpec((None, m, tn), lambda l, j: (l, 0, j))
    return pl.pallas_call(
        functools.partial(_rows_matmul_kernel, n_w=2, has_bias=False, normalized=False, swiglu=False,
                          row_chunk=min(m, MATMUL_ROW_CHUNK)),
        grid=(layers, n // tn),
        in_specs=[
            pl.BlockSpec((m, k), lambda l, j: (0, 0)),
            pl.BlockSpec((None, 1, k), lambda l, j: (l, 0, 0)),
            w_spec(), w_spec(),
        ],
        out_specs=[o_spec(), o_spec()],
        out_shape=[jax.ShapeDtypeStruct((layers, m, n), BF16)] * 2,
        scratch_shapes=[pltpu.VMEM((m, k), BF16)],
        compiler_params=_params("parallel", "arbitrary"),
        name="memory_kv",
    )(mem, gain, wk, wv)


def _ffn_down_kernel(*refs, emit):
    if emit == "none":
        a_ref, w_ref, r_ref, o_ref = refs
    elif emit == "bf16":
        a_ref, w_ref, r_ref, g_ref, o_ref, hn_ref = refs
    else:
        a_ref, w_ref, r_ref, g_ref, o_ref = refs
    a = a_ref[...]
    for c in range(o_ref.shape[1] // FFN_DOWN_COLS):
        cs = slice(c * FFN_DOWN_COLS, (c + 1) * FFN_DOWN_COLS)
        o_ref[:, cs] = r_ref[:, cs] + 0.5 * jnp.dot(a, w_ref[:, cs], preferred_element_type=F32)
    if emit == "bf16":
        hn_ref[...] = _rms(o_ref[...], g_ref[...]).astype(BF16)
    elif emit == "final":
        o_ref[...] = _rms(o_ref[...], g_ref[...])


def _ffn_down(a, w, res, gain, emit, tm):
    t, k = a.shape
    w_arr, wl = w
    n = w_arr.shape[-1]
    rows = lambda: pl.BlockSpec((tm, n), lambda i: (i, 0))
    in_specs = [
        pl.BlockSpec((tm, k), lambda i: (i, 0)),
        pl.BlockSpec((None, k, n), lambda i: (wl, 0, 0), pipeline_mode=pl.Buffered(1)),
        rows(),
    ]
    args = [a, w_arr, res]
    if emit != "none":
        g_arr, gl = gain
        in_specs.append(pl.BlockSpec((None, 1, n), lambda i: (gl, 0, 0)))
        args.append(g_arr)
    out_specs = [rows()]
    out_shape = [jax.ShapeDtypeStruct((t, n), F32)]
    if emit == "bf16":
        out_specs.append(rows())
        out_shape.append(jax.ShapeDtypeStruct((t, n), BF16))
    return pl.pallas_call(
        functools.partial(_ffn_down_kernel, emit=emit),
        grid=(t // tm,),
        in_specs=in_specs,
        out_specs=out_specs,
        out_shape=out_shape,
        compiler_params=_params("parallel"),
        name="ffn_down",
    )(*args)


def _proj_residual_kernel(*refs, has_bias):
    if has_bias:
        a_ref, w_ref, b_ref, r_ref, g_ref, o_ref, hn_ref, wb_ref = refs
    else:
        a_ref, w_ref, r_ref, g_ref, o_ref, hn_ref, wb_ref = refs

    @pl.when(pl.program_id(0) == 0)
    def _():
        wb_ref[...] = w_ref[...].astype(BF16)

    acc = jnp.dot(a_ref[...], wb_ref[...], preferred_element_type=F32)
    if has_bias:
        acc = acc + b_ref[...]
    out = r_ref[...] + acc
    o_ref[...] = out
    hn_ref[...] = _rms(out, g_ref[...]).astype(BF16)


def _proj_residual(a, w, bias, res, next_gain, tm):
    t, k = a.shape
    w_arr, wl = w
    g_arr, gl = next_gain
    n = w_arr.shape[-1]
    in_specs = [
        pl.BlockSpec((tm, k), lambda i: (i, 0)),
        pl.BlockSpec((None, k, n), lambda i: (wl, 0, 0), pipeline_mode=pl.Buffered(1)),
    ]
    args = [a, w_arr]
    if bias is not None:
        b_arr, bl = bias
        in_specs.append(pl.BlockSpec((None, 1, n), lambda i: (bl, 0, 0)))
        args.append(b_arr)
    in_specs += [pl.BlockSpec((tm, n), lambda i: (i, 0)), pl.BlockSpec((None, 1, n), lambda i: (gl, 0, 0))]
    args += [res, g_arr]
    return pl.pallas_call(
        functools.partial(_proj_residual_kernel, has_bias=bias is not None),
        grid=(t // tm,),
        in_specs=in_specs,
        out_specs=[pl.BlockSpec((tm, n), lambda i: (i, 0)), pl.BlockSpec((tm, n), lambda i: (i, 0))],
        out_shape=[jax.ShapeDtypeStruct((t, n), F32), jax.ShapeDtypeStruct((t, n), BF16)],
        scratch_shapes=[pltpu.VMEM((k, n), BF16)],
        compiler_params=_params("arbitrary"),
        name="proj_residual",
    )(*args)


def _attn_kernel(q_ref, k_ref, v_ref, o_ref, *, head_dim):
    scale = head_dim ** -0.5
    for r in range(q_ref.shape[0] // ATTN_ROW_CHUNK):
        rs = slice(r * ATTN_ROW_CHUNK, (r + 1) * ATTN_ROW_CHUNK)
        for h in range(N_MEM_HEADS):
            sl = slice(h * head_dim, (h + 1) * head_dim)
            s = lax.dot_general(q_ref[rs, sl], k_ref[:, sl], (((1,), (1,)), ((), ())),
                                preferred_element_type=F32) * scale
            e = jnp.exp(s - jnp.max(s, axis=-1, keepdims=True))
            p = e / jnp.sum(e, axis=-1, keepdims=True)
            o_ref[rs, sl] = jnp.dot(p.astype(BF16), v_ref[:, sl],
                                    preferred_element_type=F32).astype(o_ref.dtype)


def _attention(q, k, v, layer, batch, ts):
    t, d = q.shape
    m = k.shape[1] // batch
    tiles = t // batch // ts
    return pl.pallas_call(
        functools.partial(_attn_kernel, head_dim=d // N_MEM_HEADS),
        grid=(batch, tiles),
        in_specs=[
            pl.BlockSpec((ts, d), lambda b, i: (b * tiles + i, 0)),
            pl.BlockSpec((None, m, d), lambda b, i: (layer, b, 0)),
            pl.BlockSpec((None, m, d), lambda b, i: (layer, b, 0)),
        ],
        out_specs=pl.BlockSpec((ts, d), lambda b, i: (b * tiles + i, 0)),
        out_shape=jax.ShapeDtypeStruct((t, d), BF16),
        compiler_params=_params("parallel", "arbitrary"),
        name="attention",
    )(q, k, v)


def _even_mixer_kernel(z_ref, zh_ref, ws_ref, bs_ref, glg_ref, glb_ref, cw_ref, cb_ref,
                       clg_ref, clb_ref, y_ref, hh_ref, sh_ref, *, ts, width):
    i = pl.program_id(1)
    hd = width // A_HEADS

    u = jax.nn.gelu(z_ref[:, 0:width])
    v = jax.nn.gelu(z_ref[:, width:2 * width])
    vn = _layer_norm(v, glg_ref[...], glb_ref[...]).astype(BF16)
    row_chunk = lax.broadcasted_iota(jnp.int32, (GMLP_BLOCK, GMLP_BLOCK), 0) // CHUNK
    col_chunk = lax.broadcasted_iota(jnp.int32, (GMLP_BLOCK, GMLP_BLOCK), 1) // CHUNK
    causal = col_chunk <= row_chunk
    nblk = ts // GMLP_BLOCK
    for h in range(A_HEADS):
        cs = slice(h * hd, (h + 1) * hd)
        w = jnp.where(causal, ws_ref[h], 0.0).astype(BF16)
        vh = jnp.concatenate(
            [vn[n * GMLP_BLOCK:(n + 1) * GMLP_BLOCK, cs] for n in range(nblk)], axis=1)
        sp = jnp.dot(w, vh, preferred_element_type=F32) + bs_ref[:, h:h + 1]
        for n in range(nblk):
            rs = slice(n * GMLP_BLOCK, (n + 1) * GMLP_BLOCK)
            y_ref[rs, cs] = (u[rs, cs] * sp[:, n * hd:(n + 1) * hd]).astype(y_ref.dtype)

    hh_ref[CONV_HALO:, :] = z_ref[:, 2 * width:3 * width] * jax.nn.sigmoid(z_ref[:, 3 * width:4 * width])
    halo = zh_ref[:, 0:width] * jax.nn.sigmoid(zh_ref[:, width:2 * width])
    hh_ref[0:CONV_HALO, :] = jnp.where(i == 0, 0.0, halo)
    first = CONV_HALO - (CONV_WIDTH - 1)
    span = ts + CONV_HALO - SUBLANES
    acc = None
    for res in range(SUBLANES):
        offsets = [j for j in range(first, first + CONV_WIDTH) if j % SUBLANES == res]
        if res != 0:
            sh_ref[res - 1] = hh_ref[pl.ds(res, span), :]
        for j in offsets:
            if res == 0:
                tap = hh_ref[pl.ds(j, ts), :]
            else:
                tap = sh_ref[res - 1, pl.ds(j - res, ts), :]
            term = cw_ref[j - first:j - first + 1, :] * tap
            acc = term if acc is None else acc + term
    acc = acc + cb_ref[...]
    y_ref[:, width:2 * width] = _silu(_layer_norm(acc, clg_ref[...], clb_ref[...])).astype(y_ref.dtype)


def _even_mixer(z, e, w_s, b_s_t, gln_g, gln_b, conv_w, conv_b, cln_g, cln_b, batch, ts):
    _, s, zw = z.shape
    width = zw // 4
    tiles = s // ts
    halo_per_tile = ts // CONV_HALO
    vec = lambda: pl.BlockSpec((None, 1, width), lambda b, i: (e, 0, 0))
    return pl.pallas_call(
        functools.partial(_even_mixer_kernel, ts=ts, width=width),
        grid=(batch, tiles),
        in_specs=[
            pl.BlockSpec((None, ts, zw), lambda b, i: (b, i, 0)),
            pl.BlockSpec((None, CONV_HALO, 2 * width),
                         lambda b, i: (b, jnp.maximum(i * halo_per_tile - 1, 0), 1)),
            pl.BlockSpec((None, A_HEADS, GMLP_BLOCK, GMLP_BLOCK), lambda b, i: (e, 0, 0, 0)),
            pl.BlockSpec((None, GMLP_BLOCK, A_HEADS), lambda b, i: (e, 0, 0)),
            vec(), vec(),
            pl.BlockSpec((None, CONV_WIDTH, width), lambda b, i: (e, 0, 0)),
            vec(), vec(), vec(),
        ],
        out_specs=pl.BlockSpec((None, ts, 2 * width), lambda b, i: (b, i, 0)),
        out_shape=jax.ShapeDtypeStruct((batch, s, 2 * width), BF16),
        scratch_shapes=[
            pltpu.VMEM((ts + CONV_HALO, width), F32),
            pltpu.VMEM((SUBLANES - 1, ts + CONV_HALO - SUBLANES, width), F32),
        ],
        compiler_params=_params("parallel", "arbitrary"),
        name="even_mixer",
    )(z, z, w_s, b_s_t, gln_g, gln_b, conv_w, conv_b, cln_g, cln_b)


def _pool_mixer_kernel(x_ref, xh_ref, g_ref, w_ref, b_ref, sc_ref, gn_ref, o_ref, hn_ref, *, ts, group_dim):
    i = pl.program_id(1)
    g = g_ref[...]
    x = x_ref[...]
    h = _rms(x, g)
    h_halo = jnp.where(i == 0, 0.0, _rms(xh_ref[...], g))
    hist = jnp.concatenate([h_halo, h], axis=0)
    frames = i * ts + lax.broadcasted_iota(jnp.int32, (ts, 1), 0) + 1
    for gi, win in enumerate(POOL_WINDOWS):
        cs = slice(gi * group_dim, (gi + 1) * group_dim)
        wsum = hist[:, cs]
        span = 1
        while span < win:
            wsum = wsum + pltpu.roll(wsum, span, axis=0)
            span *= 2
        mean = wsum[POOL_HALO:, :] / jnp.minimum(frames, win).astype(F32)
        d = (mean - h[:, cs]).astype(BF16)
        out = jnp.dot(d, w_ref[gi].astype(BF16), preferred_element_type=F32) + b_ref[:, cs]
        o_ref[:, cs] = x[:, cs] + out * sc_ref[:, cs]
    hn_ref[...] = _rms(o_ref[...], gn_ref[...]).astype(BF16)


def _pool_mixer(x, gain, o, pool_w, pool_b, pool_scale, next_gain, batch, ts):
    _, s, d = x.shape
    g_arr, gl = gain
    gn_arr, gnl = next_gain
    groups = len(POOL_WINDOWS)
    group_dim = d // groups
    tiles = s // ts
    halo_per_tile = ts // POOL_HALO
    tile = lambda: pl.BlockSpec((None, ts, d), lambda b, i: (b, i, 0))
    return pl.pallas_call(
        functools.partial(_pool_mixer_kernel, ts=ts, group_dim=group_dim),
        grid=(batch, tiles),
        in_specs=[
            tile(),
            pl.BlockSpec((None, POOL_HALO, d), lambda b, i: (b, jnp.maximum(i * halo_per_tile - 1, 0), 0)),
            pl.BlockSpec((None, 1, d), lambda b, i: (gl, 0, 0)),
            pl.BlockSpec((None, groups, group_dim, group_dim), lambda b, i: (o, 0, 0, 0)),
            pl.BlockSpec((None, 1, d), lambda b, i: (o, 0, 0)),
            pl.BlockSpec((None, 1, d), lambda b, i: (o, 0, 0)),
            pl.BlockSpec((None, 1, d), lambda b, i: (gnl, 0, 0)),
        ],
        out_specs=[tile(), tile()],
        out_shape=[jax.ShapeDtypeStruct(x.shape, F32), jax.ShapeDtypeStruct(x.shape, BF16)],
        compiler_params=_params("parallel", "arbitrary"),
        name="pool_mixer",
    )(x, x, g_arr, pool_w, pool_b, pool_scale, gn_arr)


def kernel(x, mem, norm_ffn1, ffn1_gate, ffn1_up, ffn1_down, norm_mix, ab_w_in, ab_b_in, gmlp_w_s, gmlp_b_s, gmlp_ln_g, gmlp_ln_b, conv_w, conv_b, conv_ln_g, conv_ln_b, ab_w_out, ab_b_out, pool_w, pool_b, pool_scale, norm_xq, norm_xkv, xattn_wq, xattn_wk, xattn_wv, xattn_wo, norm_ffn2, ffn2_gate, ffn2_up, ffn2_down, norm_final):
    batch, seq, d = x.shape
    t = batch * seq
    row = lambda a: a.reshape(a.shape[0], 1, a.shape[-1])

    norm_ffn1, norm_mix, norm_xq, norm_xkv, norm_ffn2 = map(row, (norm_ffn1, norm_mix, norm_xq, norm_xkv, norm_ffn2))
    ab_b_in, ab_b_out, pool_scale = row(ab_b_in), row(ab_b_out), row(pool_scale)
    pool_b = pool_b.reshape(pool_b.shape[0], 1, d)
    gmlp_b_s_t = jnp.swapaxes(gmlp_b_s, 1, 2)
    gmlp_ln_g, gmlp_ln_b, conv_b, conv_ln_g, conv_ln_b = map(row, (gmlp_ln_g, gmlp_ln_b, conv_b, conv_ln_g, conv_ln_b))
    conv_w = conv_w.reshape(conv_w.shape[0], CONV_WIDTH, conv_w.shape[-1])
    ffn1_down, ffn2_down, ab_w_in = (w.astype(BF16) for w in (ffn1_down, ffn2_down, ab_w_in))

    xf = x.reshape(t, d)
    k_all, v_all = _memory_kv(mem.reshape(batch * N_MEM, d), norm_xkv, xattn_wk, xattn_wv, tn=512)

    norm_final = norm_final.reshape(1, 1, d)
    h1 = None
    for l in range(DEPTH):
        ffn1_ws = [(ffn1_gate, l), (ffn1_up, l)]
        if h1 is None:
            hidden, = _rows_matmul(xf, (norm_ffn1, l), ffn1_ws, None, BF16, tm=1024, tn=512, swiglu=True)
        else:
            hidden, = _rows_matmul(h1, None, ffn1_ws, None, BF16, tm=2048, tn=512, swiglu=True)
        if l % 2 == 0:
            e = l // 2
            xf, hm = _ffn_down(hidden, (ffn1_down, l), xf, (norm_mix, l), "bf16", tm=512)
            z, = _rows_matmul(hm, None, [(ab_w_in, e)], (ab_b_in, e), F32, tm=2048, tn=1024)
            y = _even_mixer(z.reshape(batch, seq, -1), e, gmlp_w_s, gmlp_b_s_t, gmlp_ln_g, gmlp_ln_b,
                            conv_w, conv_b, conv_ln_g, conv_ln_b, batch, ts=512)
            xf, hq = _proj_residual(y.reshape(t, d), (ab_w_out, e), (ab_b_out, e), xf, (norm_xq, l), tm=512)
        else:
            o = l // 2
            xf, = _ffn_down(hidden, (ffn1_down, l), xf, None, "none", tm=512)
            xf, hq = _pool_mixer(xf.reshape(batch, seq, d), (norm_mix, l), o, pool_w, pool_b, pool_scale,
                                 (norm_xq, l), batch, ts=512)
            xf, hq = xf.reshape(t, d), hq.reshape(t, d)
        q, = _rows_matmul(hq, None, [(xattn_wq, l)], None, BF16, tm=2048, tn=1024)
        att = _attention(q, k_all, v_all, l, batch, ts=seq)
        xf, hf = _proj_residual(att, (xattn_wo, l), None, xf, (norm_ffn2, l), tm=512)
        hidden, = _rows_matmul(hf, None, [(ffn2_gate, l), (ffn2_up, l)], None, BF16,
                               tm=2048, tn=512, swiglu=True)
        if l + 1 < DEPTH:
            xf, h1 = _ffn_down(hidden, (ffn2_down, l), xf, (norm_ffn1, l + 1), "bf16", tm=512)
        else:
            out, = _ffn_down(hidden, (ffn2_down, l), xf, (norm_final, 0), "final", tm=512)
    return out.reshape(batch, seq, d)
```

```python
import functools

import jax
import jax.numpy as jnp
from jax import lax
from jax.experimental import pallas as pl
from jax.experimental.pallas import tpu as pltpu

F32 = jnp.float32
BF16 = jnp.bfloat16

DEPTH = 4
CHUNK = 64
N_MEM = 256
N_MEM_HEADS = 4
A_HEADS = 8
GMLP_BLOCK = 128
CONV_WIDTH = 31
POOL_WINDOWS = (2, 4, 8, 16)
EPS = 1e-6

SUBLANES = 8
BF16_SUBLANES = 16
MXU_COLS = 256
MATMUL_ROW_CHUNK = 1024
ATTN_ROW_CHUNK = 512
FFN_DOWN_COLS = 512
CONV_HALO = 32
POOL_HALO = 16

VMEM_LIMIT_BYTES = 58 * 1024 * 1024


def _params(*semantics):
    return pltpu.CompilerParams(dimension_semantics=semantics, vmem_limit_bytes=VMEM_LIMIT_BYTES)


def _rms(x, g):
    return x * lax.rsqrt(jnp.mean(x * x, axis=-1, keepdims=True) + EPS) * g


def _layer_norm(x, g, b):
    mu = jnp.mean(x, axis=-1, keepdims=True)
    xc = x - mu
    var = jnp.mean(xc * xc, axis=-1, keepdims=True)
    return xc * lax.rsqrt(var + EPS) * g + b


def _silu(x):
    return x * jax.nn.sigmoid(x)


def _rows_matmul_kernel(*refs, n_w, has_bias, normalized, swiglu, row_chunk, rounds_weight):
    refs = list(refs)
    x_ref = refs.pop(0)
    g_ref = None if normalized else refs.pop(0)
    w_refs = [refs.pop(0) for _ in range(n_w)]
    b_ref = refs.pop(0) if has_bias else None
    wide_ref = refs.pop(0) if rounds_weight else None
    o_refs = [refs.pop(0) for _ in range(1 if swiglu else n_w)]
    if rounds_weight:
        refs.pop(0)[...] = wide_ref[...].astype(BF16)

    if normalized:
        h_ref = x_ref
    else:
        h_ref, = refs

        @pl.when(pl.program_id(1) == 0)
        def _():
            h_ref[...] = _rms(x_ref[...], g_ref[...]).astype(BF16)

    tm, tn = o_refs[0].shape
    for c in range(tn // MXU_COLS):
        cs = slice(c * MXU_COLS, (c + 1) * MXU_COLS)
        w_cols = [w_ref[:, cs].astype(BF16) for w_ref in w_refs]
        for r in range(tm // row_chunk):
            rs = slice(r * row_chunk, (r + 1) * row_chunk)
            accs = [jnp.dot(h_ref[rs, :], w, preferred_element_type=F32) for w in w_cols]
            if swiglu:
                o_refs[0][rs, cs] = (_silu(accs[0]) * accs[1]).astype(o_refs[0].dtype)
            else:
                for acc, o_ref in zip(accs, o_refs):
                    if has_bias:
                        acc = acc + b_ref[:, cs]
                    o_ref[rs, cs] = acc.astype(o_ref.dtype)


def _rows_matmul(x, gain, ws, bias, out_dtype, tm, tn, swiglu=False, round_weight=None):
    t, k = x.shape
    n = ws[0][0].shape[-1]
    normalized = gain is None
    steps_j = n // tn
    in_specs = [pl.BlockSpec((tm, k), lambda i, j: (i, 0))]
    args = [x]
    if not normalized:
        g_arr, gl = gain
        in_specs.append(pl.BlockSpec((None, 1, k), lambda i, j: (gl, 0, 0)))
        args.append(g_arr)
    for w_arr, wl in ws:
        in_specs.append(pl.BlockSpec((None, k, tn), lambda i, j, wl=wl: (wl, 0, j)))
        args.append(w_arr)
    if bias is not None:
        b_arr, bl = bias
        in_specs.append(pl.BlockSpec((None, 1, tn), lambda i, j: (bl, 0, j)))
        args.append(b_arr)
    n_out = 1 if swiglu else len(ws)
    out_specs = [pl.BlockSpec((tm, tn), lambda i, j: (i, j)) for _ in range(n_out)]
    out_shape = [jax.ShapeDtypeStruct((t, n), out_dtype) for _ in range(n_out)]
    if round_weight is not None:
        wide, wide_l = round_weight
        _, wide_rows, wide_cols = wide.shape
        slab = wide_rows // ((t // tm) * steps_j)
        assert slab * (t // tm) * steps_j == wide_rows and slab % BF16_SUBLANES == 0
        in_specs.append(pl.BlockSpec((None, slab, wide_cols), lambda i, j: (wide_l, i * steps_j + j, 0)))
        args.append(wide)
        out_specs.append(pl.BlockSpec((slab, wide_cols), lambda i, j: (i * steps_j + j, 0)))
        out_shape.append(jax.ShapeDtypeStruct((wide_rows, wide_cols), BF16))
    return pl.pallas_call(
        functools.partial(_rows_matmul_kernel, n_w=len(ws), has_bias=bias is not None,
                          normalized=normalized, swiglu=swiglu, row_chunk=min(tm, MATMUL_ROW_CHUNK),
                          rounds_weight=round_weight is not None),
        grid=(t // tm, steps_j),
        in_specs=in_specs,
        out_specs=out_specs,
        out_shape=out_shape,
        scratch_shapes=[] if normalized else [pltpu.VMEM((tm, k), BF16)],
        compiler_params=_params("parallel", "arbitrary"),
        name="swiglu" if swiglu else "rows_matmul",
    )(*args)


def _memory_kv(mem, gain, wk, wv, tn):
    m, k = mem.shape
    layers, _, n = wk.shape
    w_spec = lambda: pl.BlockSpec((None, k, tn), lambda l, j: (l, 0, j))
    o_spec = lambda: pl.BlockSpec((None, m, tn), lambda l, j: (l, 0, j))
    return pl.pallas_call(
        functools.partial(_rows_matmul_kernel, n_w=2, has_bias=False, normalized=False, swiglu=False,
                          row_chunk=min(m, MATMUL_ROW_CHUNK), rounds_weight=False),
        grid=(layers, n // tn),
        in_specs=[
            pl.BlockSpec((m, k), lambda l, j: (0, 0)),
            pl.BlockSpec((None, 1, k), lambda l, j: (l, 0, 0)),
            w_spec(), w_spec(),
        ],
        out_specs=[o_spec(), o_spec()],
        out_shape=[jax.ShapeDtypeStruct((layers, m, n), BF16)] * 2,
        scratch_shapes=[pltpu.VMEM((m, k), BF16)],
        compiler_params=_params("parallel", "arbitrary"),
        name="memory_kv",
    )(mem, gain, wk, wv)


def _ffn_down_kernel(*refs, emit):
    if emit == "none":
        a_ref, w_ref, r_ref, o_ref = refs
    elif emit == "bf16":
        a_ref, w_ref, r_ref, g_ref, o_ref, hn_ref = refs
    else:
        a_ref, w_ref, r_ref, g_ref, o_ref = refs
    a = a_ref[...]
    for c in range(o_ref.shape[1] // FFN_DOWN_COLS):
        cs = slice(c * FFN_DOWN_COLS, (c + 1) * FFN_DOWN_COLS)
        o_ref[:, cs] = r_ref[:, cs] + 0.5 * jnp.dot(a, w_ref[:, cs], preferred_element_type=F32)
    if emit == "bf16":
        hn_ref[...] = _rms(o_ref[...], g_ref[...]).astype(BF16)
    elif emit == "final":
        o_ref[...] = _rms(o_ref[...], g_ref[...])


def _ffn_down(a, w, res, gain, emit, tm):
    t, k = a.shape
    n = w.shape[-1]
    rows = lambda: pl.BlockSpec((tm, n), lambda i: (i, 0))
    in_specs = [
        pl.BlockSpec((tm, k), lambda i: (i, 0)),
        pl.BlockSpec((k, n), lambda i: (0, 0), pipeline_mode=pl.Buffered(1)),
        rows(),
    ]
    args = [a, w, res]
    if emit != "none":
        g_arr, gl = gain
        in_specs.append(pl.BlockSpec((None, 1, n), lambda i: (gl, 0, 0)))
        args.append(g_arr)
    out_specs = [rows()]
    out_shape = [jax.ShapeDtypeStruct((t, n), F32)]
    if emit == "bf16":
        out_specs.append(rows())
        out_shape.append(jax.ShapeDtypeStruct((t, n), BF16))
    return pl.pallas_call(
        functools.partial(_ffn_down_kernel, emit=emit),
        grid=(t // tm,),
        in_specs=in_specs,
        out_specs=out_specs,
        out_shape=out_shape,
        compiler_params=_params("parallel"),
        name="ffn_down",
    )(*args)


def _proj_residual_kernel(*refs, has_bias):
    if has_bias:
        a_ref, w_ref, b_ref, r_ref, g_ref, o_ref, hn_ref, wb_ref = refs
    else:
        a_ref, w_ref, r_ref, g_ref, o_ref, hn_ref, wb_ref = refs

    @pl.when(pl.program_id(0) == 0)
    def _():
        wb_ref[...] = w_ref[...].astype(BF16)

    acc = jnp.dot(a_ref[...], wb_ref[...], preferred_element_type=F32)
    if has_bias:
        acc = acc + b_ref[...]
    out = r_ref[...] + acc
    o_ref[...] = out
    hn_ref[...] = _rms(out, g_ref[...]).astype(BF16)


def _proj_residual(a, w, bias, res, next_gain, tm):
    t, k = a.shape
    w_arr, wl = w
    g_arr, gl = next_gain
    n = w_arr.shape[-1]
    in_specs = [
        pl.BlockSpec((tm, k), lambda i: (i, 0)),
        pl.BlockSpec((None, k, n), lambda i: (wl, 0, 0), pipeline_mode=pl.Buffered(1)),
    ]
    args = [a, w_arr]
    if bias is not None:
        b_arr, bl = bias
        in_specs.append(pl.BlockSpec((None, 1, n), lambda i: (bl, 0, 0)))
        args.append(b_arr)
    in_specs += [pl.BlockSpec((tm, n), lambda i: (i, 0)), pl.BlockSpec((None, 1, n), lambda i: (gl, 0, 0))]
    args += [res, g_arr]
    return pl.pallas_call(
        functools.partial(_proj_residual_kernel, has_bias=bias is not None),
        grid=(t // tm,),
        in_specs=in_specs,
        out_specs=[pl.BlockSpec((tm, n), lambda i: (i, 0)), pl.BlockSpec((tm, n), lambda i: (i, 0))],
        out_shape=[jax.ShapeDtypeStruct((t, n), F32), jax.ShapeDtypeStruct((t, n), BF16)],
        scratch_shapes=[pltpu.VMEM((k, n), BF16)],
        compiler_params=_params("arbitrary"),
        name="proj_residual",
    )(*args)


def _attn_kernel(q_ref, k_ref, v_ref, o_ref, *, head_dim):
    scale = head_dim ** -0.5
    for r in range(q_ref.shape[0] // ATTN_ROW_CHUNK):
        rs = slice(r * ATTN_ROW_CHUNK, (r + 1) * ATTN_ROW_CHUNK)
        for h in range(N_MEM_HEADS):
            sl = slice(h * head_dim, (h + 1) * head_dim)
            s = lax.dot_general(q_ref[rs, sl], k_ref[:, sl], (((1,), (1,)), ((), ())),
                                preferred_element_type=F32) * scale
            e = jnp.exp(s - jnp.max(s, axis=-1, keepdims=True))
            p = e / jnp.sum(e, axis=-1, keepdims=True)
            o_ref[rs, sl] = jnp.dot(p.astype(BF16), v_ref[:, sl],
                                    preferred_element_type=F32).astype(o_ref.dtype)


def _attention(q, k, v, layer, batch, ts):
    t, d = q.shape
    m = k.shape[1] // batch
    tiles = t // batch // ts
    return pl.pallas_call(
        functools.partial(_attn_kernel, head_dim=d // N_MEM_HEADS),
        grid=(batch, tiles),
        in_specs=[
            pl.BlockSpec((ts, d), lambda b, i: (b * tiles + i, 0)),
            pl.BlockSpec((None, m, d), lambda b, i: (layer, b, 0)),
            pl.BlockSpec((None, m, d), lambda b, i: (layer, b, 0)),
        ],
        out_specs=pl.BlockSpec((ts, d), lambda b, i: (b * tiles + i, 0)),
        out_shape=jax.ShapeDtypeStruct((t, d), BF16),
        compiler_params=_params("parallel", "arbitrary"),
        name="attention",
    )(q, k, v)


def _even_mixer_kernel(z_ref, zh_ref, ws_ref, bs_ref, glg_ref, glb_ref, cw_ref, cb_ref,
                       clg_ref, clb_ref, y_ref, hh_ref, sh_ref, *, ts, width):
    i = pl.program_id(1)
    hd = width // A_HEADS

    u = jax.nn.gelu(z_ref[:, 0:width])
    v = jax.nn.gelu(z_ref[:, width:2 * width])
    vn = _layer_norm(v, glg_ref[...], glb_ref[...]).astype(BF16)
    row_chunk = lax.broadcasted_iota(jnp.int32, (GMLP_BLOCK, GMLP_BLOCK), 0) // CHUNK
    col_chunk = lax.broadcasted_iota(jnp.int32, (GMLP_BLOCK, GMLP_BLOCK), 1) // CHUNK
    causal = col_chunk <= row_chunk
    nblk = ts // GMLP_BLOCK
    for h in range(A_HEADS):
        cs = slice(h * hd, (h + 1) * hd)
        w = jnp.where(causal, ws_ref[h], 0.0).astype(BF16)
        vh = jnp.concatenate(
            [vn[n * GMLP_BLOCK:(n + 1) * GMLP_BLOCK, cs] for n in range(nblk)], axis=1)
        sp = jnp.dot(w, vh, preferred_element_type=F32) + bs_ref[:, h:h + 1]
        for n in range(nblk):
            rs = slice(n * GMLP_BLOCK, (n + 1) * GMLP_BLOCK)
            y_ref[rs, cs] = (u[rs, cs] * sp[:, n * hd:(n + 1) * hd]).astype(y_ref.dtype)

    hh_ref[CONV_HALO:, :] = z_ref[:, 2 * width:3 * width] * jax.nn.sigmoid(z_ref[:, 3 * width:4 * width])
    halo = zh_ref[:, 0:width] * jax.nn.sigmoid(zh_ref[:, width:2 * width])
    hh_ref[0:CONV_HALO, :] = jnp.where(i == 0, 0.0, halo)
    first = CONV_HALO - (CONV_WIDTH - 1)
    span = ts + CONV_HALO - SUBLANES
    acc = None
    for res in range(SUBLANES):
        offsets = [j for j in range(first, first + CONV_WIDTH) if j % SUBLANES == res]
        if res != 0:
            sh_ref[res - 1] = hh_ref[pl.ds(res, span), :]
        for j in offsets:
            if res == 0:
                tap = hh_ref[pl.ds(j, ts), :]
            else:
                tap = sh_ref[res - 1, pl.ds(j - res, ts), :]
            term = cw_ref[j - first:j - first + 1, :] * tap
            acc = term if acc is None else acc + term
    acc = acc + cb_ref[...]
    y_ref[:, width:2 * width] = _silu(_layer_norm(acc, clg_ref[...], clb_ref[...])).astype(y_ref.dtype)


def _even_mixer(z, e, w_s, b_s_t, gln_g, gln_b, conv_w, conv_b, cln_g, cln_b, batch, ts):
    _, s, zw = z.shape
    width = zw // 4
    tiles = s // ts
    halo_per_tile = ts // CONV_HALO
    vec = lambda: pl.BlockSpec((None, 1, width), lambda b, i: (e, 0, 0))
    return pl.pallas_call(
        functools.partial(_even_mixer_kernel, ts=ts, width=width),
        grid=(batch, tiles),
        in_specs=[
            pl.BlockSpec((None, ts, zw), lambda b, i: (b, i, 0)),
            pl.BlockSpec((None, CONV_HALO, 2 * width),
                         lambda b, i: (b, jnp.maximum(i * halo_per_tile - 1, 0), 1)),
            pl.BlockSpec((None, A_HEADS, GMLP_BLOCK, GMLP_BLOCK), lambda b, i: (e, 0, 0, 0)),
            pl.BlockSpec((None, GMLP_BLOCK, A_HEADS), lambda b, i: (e, 0, 0)),
            vec(), vec(),
            pl.BlockSpec((None, CONV_WIDTH, width), lambda b, i: (e, 0, 0)),
            vec(), vec(), vec(),
        ],
        out_specs=pl.BlockSpec((None, ts, 2 * width), lambda b, i: (b, i, 0)),
        out_shape=jax.ShapeDtypeStruct((batch, s, 2 * width), BF16),
        scratch_shapes=[
            pltpu.VMEM((ts + CONV_HALO, width), F32),
            pltpu.VMEM((SUBLANES - 1, ts + CONV_HALO - SUBLANES, width), F32),
        ],
        compiler_params=_params("parallel", "arbitrary"),
        name="even_mixer",
    )(z, z, w_s, b_s_t, gln_g, gln_b, conv_w, conv_b, cln_g, cln_b)


def _pool_mixer_kernel(x_ref, xh_ref, g_ref, w_ref, b_ref, sc_ref, gn_ref, o_ref, hn_ref, *, ts, group_dim):
    i = pl.program_id(1)
    g = g_ref[...]
    x = x_ref[...]
    h = _rms(x, g)
    h_halo = jnp.where(i == 0, 0.0, _rms(xh_ref[...], g))
    hist = jnp.concatenate([h_halo, h], axis=0)
    frames = i * ts + lax.broadcasted_iota(jnp.int32, (ts, 1), 0) + 1
    for gi, win in enumerate(POOL_WINDOWS):
        cs = slice(gi * group_dim, (gi + 1) * group_dim)
        wsum = hist[:, cs]
        span = 1
        while span < win:
            wsum = wsum + pltpu.roll(wsum, span, axis=0)
            span *= 2
        mean = wsum[POOL_HALO:, :] / jnp.minimum(frames, win).astype(F32)
        d = (mean - h[:, cs]).astype(BF16)
        out = jnp.dot(d, w_ref[gi].astype(BF16), preferred_element_type=F32) + b_ref[:, cs]
        o_ref[:, cs] = x[:, cs] + out * sc_ref[:, cs]
    hn_ref[...] = _rms(o_ref[...], gn_ref[...]).astype(BF16)


def _pool_mixer(x, gain, o, pool_w, pool_b, pool_scale, next_gain, batch, ts):
    _, s, d = x.shape
    g_arr, gl = gain
    gn_arr, gnl = next_gain
    groups = len(POOL_WINDOWS)
    group_dim = d // groups
    tiles = s // ts
    halo_per_tile = ts // POOL_HALO
    tile = lambda: pl.BlockSpec((None, ts, d), lambda b, i: (b, i, 0))
    return pl.pallas_call(
        functools.partial(_pool_mixer_kernel, ts=ts, group_dim=group_dim),
        grid=(batch, tiles),
        in_specs=[
            tile(),
            pl.BlockSpec((None, POOL_HALO, d), lambda b, i: (b, jnp.maximum(i * halo_per_tile - 1, 0), 0)),
            pl.BlockSpec((None, 1, d), lambda b, i: (gl, 0, 0)),
            pl.BlockSpec((None, groups, group_dim, group_dim), lambda b, i: (o, 0, 0, 0)),
            pl.BlockSpec((None, 1, d), lambda b, i: (o, 0, 0)),
            pl.BlockSpec((None, 1, d), lambda b, i: (o, 0, 0)),
            pl.BlockSpec((None, 1, d), lambda b, i: (gnl, 0, 0)),
        ],
        out_specs=[tile(), tile()],
        out_shape=[jax.ShapeDtypeStruct(x.shape, F32), jax.ShapeDtypeStruct(x.shape, BF16)],
        compiler_params=_params("parallel", "arbitrary"),
        name="pool_mixer",
    )(x, x, g_arr, pool_w, pool_b, pool_scale, gn_arr)


def kernel(x, mem, norm_ffn1, ffn1_gate, ffn1_up, ffn1_down, norm_mix, ab_w_in, ab_b_in, gmlp_w_s, gmlp_b_s, gmlp_ln_g, gmlp_ln_b, conv_w, conv_b, conv_ln_g, conv_ln_b, ab_w_out, ab_b_out, pool_w, pool_b, pool_scale, norm_xq, norm_xkv, xattn_wq, xattn_wk, xattn_wv, xattn_wo, norm_ffn2, ffn2_gate, ffn2_up, ffn2_down, norm_final):
    batch, seq, d = x.shape
    t = batch * seq
    row = lambda a: a.reshape(a.shape[0], 1, a.shape[-1])

    norm_ffn1, norm_mix, norm_xq, norm_xkv, norm_ffn2 = map(row, (norm_ffn1, norm_mix, norm_xq, norm_xkv, norm_ffn2))
    ab_b_in, ab_b_out, pool_scale = row(ab_b_in), row(ab_b_out), row(pool_scale)
    pool_b = pool_b.reshape(pool_b.shape[0], 1, d)
    gmlp_b_s_t = jnp.swapaxes(gmlp_b_s, 1, 2)
    gmlp_ln_g, gmlp_ln_b, conv_b, conv_ln_g, conv_ln_b = map(row, (gmlp_ln_g, gmlp_ln_b, conv_b, conv_ln_g, conv_ln_b))
    conv_w = conv_w.reshape(conv_w.shape[0], CONV_WIDTH, conv_w.shape[-1])
    xf = x.reshape(t, d)
    k_all, v_all = _memory_kv(mem.reshape(batch * N_MEM, d), norm_xkv, xattn_wk, xattn_wv, tn=512)

    norm_final = norm_final.reshape(1, 1, d)
    h1 = None
    for l in range(DEPTH):
        ffn1_ws = [(ffn1_gate, l), (ffn1_up, l)]
        if h1 is None:
            hidden, w_down = _rows_matmul(xf, (norm_ffn1, l), ffn1_ws, None, BF16, tm=1024, tn=512,
                                          swiglu=True, round_weight=(ffn1_down, l))
        else:
            hidden, w_down = _rows_matmul(h1, None, ffn1_ws, None, BF16, tm=2048, tn=512,
                                          swiglu=True, round_weight=(ffn1_down, l))
        if l % 2 == 0:
            e = l // 2
            xf, hm = _ffn_down(hidden, w_down, xf, (norm_mix, l), "bf16", tm=512)
            z, = _rows_matmul(hm, None, [(ab_w_in, e)], (ab_b_in, e), F32, tm=2048, tn=1024)
            y = _even_mixer(z.reshape(batch, seq, -1), e, gmlp_w_s, gmlp_b_s_t, gmlp_ln_g, gmlp_ln_b,
                            conv_w, conv_b, conv_ln_g, conv_ln_b, batch, ts=512)
            xf, hq = _proj_residual(y.reshape(t, d), (ab_w_out, e), (ab_b_out, e), xf, (norm_xq, l), tm=512)
        else:
            o = l // 2
            xf, = _ffn_down(hidden, w_down, xf, None, "none", tm=512)
            xf, hq = _pool_mixer(xf.reshape(batch, seq, d), (norm_mix, l), o, pool_w, pool_b, pool_scale,
                                 (norm_xq, l), batch, ts=512)
            xf, hq = xf.reshape(t, d), hq.reshape(t, d)
        q, = _rows_matmul(hq, None, [(xattn_wq, l)], None, BF16, tm=2048, tn=1024)
        att = _attention(q, k_all, v_all, l, batch, ts=seq)
        xf, hf = _proj_residual(att, (xattn_wo, l), None, xf, (norm_ffn2, l), tm=512)
        hidden, w_down = _rows_matmul(hf, None, [(ffn2_gate, l), (ffn2_up, l)], None, BF16, tm=2048, tn=512,
                                      swiglu=True, round_weight=(ffn2_down, l))
        if l + 1 < DEPTH:
            xf, h1 = _ffn_down(hidden, w_down, xf, (norm_ffn1, l + 1), "bf16", tm=512)
        else:
            out, = _ffn_down(hidden, w_down, xf, (norm_final, 0), "final", tm=512)
    return out.reshape(batch, seq, d)
```

```python
import functools

import jax
import jax.numpy as jnp
from jax import lax
from jax.experimental import pallas as pl
from jax.experimental.pallas import tpu as pltpu

F32 = jnp.float32
BF16 = jnp.bfloat16

DEPTH = 4
CHUNK = 64
N_MEM = 256
N_MEM_HEADS = 4
A_HEADS = 8
GMLP_BLOCK = 128
CONV_WIDTH = 31
POOL_WINDOWS = (2, 4, 8, 16)
EPS = 1e-6

LANES = 128
SUBLANES = 8
BF16_SUBLANES = 16
MXU_COLS = 256
MATMUL_ROW_CHUNK = 1024
ATTN_ROW_CHUNK = 512
FFN_DOWN_COLS = 512
CONV_HALO = 32
POOL_HALO = 16

ROW_TILE_BF16 = 2048
ROW_TILE_F32 = 512
SWIGLU_COLS = 512
PROJ_COLS = 1024

VMEM_LIMIT_BYTES = 58 * 1024 * 1024


def _params(*semantics):
    return pltpu.CompilerParams(dimension_semantics=semantics, vmem_limit_bytes=VMEM_LIMIT_BYTES)


def _rms(x, g):
    return x * lax.rsqrt(jnp.mean(x * x, axis=-1, keepdims=True) + EPS) * g


def _layer_norm(x, g, b):
    mu = jnp.mean(x, axis=-1, keepdims=True)
    xc = x - mu
    var = jnp.mean(xc * xc, axis=-1, keepdims=True)
    return xc * lax.rsqrt(var + EPS) * g + b


def _silu(x):
    return x * jax.nn.sigmoid(x)


def _rows_matmul_kernel(*refs, n_w, has_bias, normalized, swiglu, row_chunk, side_job):
    refs = list(refs)
    x_ref = refs.pop(0)
    g_ref = None if normalized else refs.pop(0)
    w_refs = [refs.pop(0) for _ in range(n_w)]
    b_ref = refs.pop(0) if has_bias else None
    side_in = [refs.pop(0) for _ in range({None: 0, "round": 1, "rmsnorm": 2}[side_job])]
    o_refs = [refs.pop(0) for _ in range(1 if swiglu else n_w)]
    if side_job == "round":
        refs.pop(0)[...] = side_in[0][...].astype(BF16)
    elif side_job == "rmsnorm":
        refs.pop(0)[...] = _rms(side_in[0][...], side_in[1][...]).astype(BF16)

    if normalized:
        h_ref = x_ref
    else:
        h_ref, = refs

        @pl.when(pl.program_id(1) == 0)
        def _():
            h_ref[...] = _rms(x_ref[...], g_ref[...]).astype(BF16)

    tm, tn = o_refs[0].shape
    for c in range(tn // MXU_COLS):
        cs = slice(c * MXU_COLS, (c + 1) * MXU_COLS)
        w_cols = [w_ref[:, cs].astype(BF16) for w_ref in w_refs]
        for r in range(tm // row_chunk):
            rs = slice(r * row_chunk, (r + 1) * row_chunk)
            accs = [jnp.dot(h_ref[rs, :], w, preferred_element_type=F32) for w in w_cols]
            if swiglu:
                o_refs[0][rs, cs] = (_silu(accs[0]) * accs[1]).astype(o_refs[0].dtype)
            else:
                for acc, o_ref in zip(accs, o_refs):
                    if has_bias:
                        acc = acc + b_ref[:, cs]
                    o_ref[rs, cs] = acc.astype(o_ref.dtype)


def _rows_matmul(h, ws, bias, out_dtype, tn, swiglu=False, round_weight=None):
    t, k = h.shape
    n = ws[0][0].shape[-1]
    tm = ROW_TILE_BF16
    steps_j = n // tn
    in_specs = [pl.BlockSpec((tm, k), lambda i, j: (i, 0))]
    args = [h]
    for w_arr, wl in ws:
        in_specs.append(pl.BlockSpec((None, k, tn), lambda i, j, wl=wl: (wl, 0, j)))
        args.append(w_arr)
    if bias is not None:
        b_arr, bl = bias
        in_specs.append(pl.BlockSpec((None, 1, tn), lambda i, j: (bl, 0, j)))
        args.append(b_arr)
    n_out = 1 if swiglu else len(ws)
    out_specs = [pl.BlockSpec((tm, tn), lambda i, j: (i, j)) for _ in range(n_out)]
    out_shape = [jax.ShapeDtypeStruct((t, n), out_dtype) for _ in range(n_out)]
    if round_weight is not None:
        wide, wide_l = round_weight
        _, wide_rows, wide_cols = wide.shape
        slab = wide_rows // ((t // tm) * steps_j)
        assert slab * (t // tm) * steps_j == wide_rows and slab % BF16_SUBLANES == 0
        in_specs.append(pl.BlockSpec((None, slab, wide_cols), lambda i, j: (wide_l, i * steps_j + j, 0)))
        args.append(wide)
        out_specs.append(pl.BlockSpec((slab, wide_cols), lambda i, j: (i * steps_j + j, 0)))
        out_shape.append(jax.ShapeDtypeStruct((wide_rows, wide_cols), BF16))
    return pl.pallas_call(
        functools.partial(_rows_matmul_kernel, n_w=len(ws), has_bias=bias is not None,
                          normalized=True, swiglu=swiglu, row_chunk=MATMUL_ROW_CHUNK,
                          side_job=None if round_weight is None else "round"),
        grid=(t // tm, steps_j),
        in_specs=in_specs,
        out_specs=out_specs,
        out_shape=out_shape,
        compiler_params=_params("parallel", "arbitrary"),
        name="swiglu" if swiglu else "rows_matmul",
    )(*args)


def _memory_kv(mem, gain, wk, wv, x, x_gain, tn):
    m, k = mem.shape
    layers, _, n = wk.shape
    steps_j = n // tn
    t = x.shape[0]
    slab = t // (layers * steps_j)
    assert slab * layers * steps_j == t and slab % BF16_SUBLANES == 0
    w_spec = lambda: pl.BlockSpec((None, k, tn), lambda l, j: (l, 0, j))
    o_spec = lambda: pl.BlockSpec((None, m, tn), lambda l, j: (l, 0, j))
    x_spec = lambda: pl.BlockSpec((slab, k), lambda l, j: (l * steps_j + j, 0))
    return pl.pallas_call(
        functools.partial(_rows_matmul_kernel, n_w=2, has_bias=False, normalized=False, swiglu=False,
                          row_chunk=min(m, MATMUL_ROW_CHUNK), side_job="rmsnorm"),
        grid=(layers, steps_j),
        in_specs=[
            pl.BlockSpec((m, k), lambda l, j: (0, 0)),
            pl.BlockSpec((None, 1, k), lambda l, j: (l, 0, 0)),
            w_spec(), w_spec(),
            x_spec(),
            pl.BlockSpec((None, 1, k), lambda l, j: (0, 0, 0)),
        ],
        out_specs=[o_spec(), o_spec(), x_spec()],
        out_shape=[jax.ShapeDtypeStruct((layers, m, n), BF16)] * 2 + [jax.ShapeDtypeStruct((t, k), BF16)],
        scratch_shapes=[pltpu.VMEM((m, k), BF16)],
        compiler_params=_params("parallel", "arbitrary"),
        name="memory_kv",
    )(mem, gain, wk, wv, x, x_gain)


def _ffn_down_kernel(*refs, emit):
    if emit == "none":
        a_ref, w_ref, r_ref, o_ref = refs
    elif emit == "bf16":
        a_ref, w_ref, r_ref, g_ref, o_ref, hn_ref = refs
    else:
        a_ref, w_ref, r_ref, g_ref, o_ref = refs
    a = a_ref[...]
    for c in range(o_ref.shape[1] // FFN_DOWN_COLS):
        cs = slice(c * FFN_DOWN_COLS, (c + 1) * FFN_DOWN_COLS)
        o_ref[:, cs] = r_ref[:, cs] + 0.5 * jnp.dot(a, w_ref[:, cs], preferred_element_type=F32)
    if emit == "bf16":
        hn_ref[...] = _rms(o_ref[...], g_ref[...]).astype(BF16)
    elif emit == "final":
        o_ref[...] = _rms(o_ref[...], g_ref[...])


def _ffn_down(a, w, res, gain, emit, tm):
    t, k = a.shape
    n = w.shape[-1]
    rows = lambda: pl.BlockSpec((tm, n), lambda i: (i, 0))
    in_specs = [
        pl.BlockSpec((tm, k), lambda i: (i, 0)),
        pl.BlockSpec((k, n), lambda i: (0, 0), pipeline_mode=pl.Buffered(1)),
        rows(),
    ]
    args = [a, w, res]
    if emit != "none":
        g_arr, gl = gain
        in_specs.append(pl.BlockSpec((None, 1, n), lambda i: (gl, 0, 0)))
        args.append(g_arr)
    out_specs = [rows()]
    out_shape = [jax.ShapeDtypeStruct((t, n), F32)]
    if emit == "bf16":
        out_specs.append(rows())
        out_shape.append(jax.ShapeDtypeStruct((t, n), BF16))
    return pl.pallas_call(
        functools.partial(_ffn_down_kernel, emit=emit),
        grid=(t // tm,),
        in_specs=in_specs,
        out_specs=out_specs,
        out_shape=out_shape,
        compiler_params=_params("parallel"),
        name="ffn_down",
    )(*args)


def _proj_residual_kernel(*refs, has_bias):
    if has_bias:
        a_ref, w_ref, b_ref, r_ref, g_ref, o_ref, hn_ref, wb_ref = refs
    else:
        a_ref, w_ref, r_ref, g_ref, o_ref, hn_ref, wb_ref = refs

    @pl.when(pl.program_id(0) == 0)
    def _():
        wb_ref[...] = w_ref[...].astype(BF16)

    acc = jnp.dot(a_ref[...], wb_ref[...], preferred_element_type=F32)
    if has_bias:
        acc = acc + b_ref[...]
    out = r_ref[...] + acc
    o_ref[...] = out
    hn_ref[...] = _rms(out, g_ref[...]).astype(BF16)


def _proj_residual(a, w, bias, res, next_gain, tm):
    t, k = a.shape
    w_arr, wl = w
    g_arr, gl = next_gain
    n = w_arr.shape[-1]
    in_specs = [
        pl.BlockSpec((tm, k), lambda i: (i, 0)),
        pl.BlockSpec((None, k, n), lambda i: (wl, 0, 0), pipeline_mode=pl.Buffered(1)),
    ]
    args = [a, w_arr]
    if bias is not None:
        b_arr, bl = bias
        in_specs.append(pl.BlockSpec((None, 1, n), lambda i: (bl, 0, 0)))
        args.append(b_arr)
    in_specs += [pl.BlockSpec((tm, n), lambda i: (i, 0)), pl.BlockSpec((None, 1, n), lambda i: (gl, 0, 0))]
    args += [res, g_arr]
    return pl.pallas_call(
        functools.partial(_proj_residual_kernel, has_bias=bias is not None),
        grid=(t // tm,),
        in_specs=in_specs,
        out_specs=[pl.BlockSpec((tm, n), lambda i: (i, 0)), pl.BlockSpec((tm, n), lambda i: (i, 0))],
        out_shape=[jax.ShapeDtypeStruct((t, n), F32), jax.ShapeDtypeStruct((t, n), BF16)],
        scratch_shapes=[pltpu.VMEM((k, n), BF16)],
        compiler_params=_params("arbitrary"),
        name="proj_residual",
    )(*args)


def _attn_kernel(q_ref, k_ref, v_ref, o_ref, *, head_dim):
    scale = head_dim ** -0.5
    for r in range(q_ref.shape[0] // ATTN_ROW_CHUNK):
        rs = slice(r * ATTN_ROW_CHUNK, (r + 1) * ATTN_ROW_CHUNK)
        for h in range(N_MEM_HEADS):
            sl = slice(h * head_dim, (h + 1) * head_dim)
            s = lax.dot_general(q_ref[rs, sl], k_ref[:, sl], (((1,), (1,)), ((), ())),
                                preferred_element_type=F32) * scale
            e = jnp.exp(s - jnp.max(s, axis=-1, keepdims=True))
            p = e / jnp.sum(e, axis=-1, keepdims=True)
            o_ref[rs, sl] = jnp.dot(p.astype(BF16), v_ref[:, sl],
                                    preferred_element_type=F32).astype(o_ref.dtype)


def _attention(q, k, v, layer, batch, ts):
    t, d = q.shape
    m = k.shape[1] // batch
    tiles = t // batch // ts
    return pl.pallas_call(
        functools.partial(_attn_kernel, head_dim=d // N_MEM_HEADS),
        grid=(batch, tiles),
        in_specs=[
            pl.BlockSpec((ts, d), lambda b, i: (b * tiles + i, 0)),
            pl.BlockSpec((None, m, d), lambda b, i: (layer, b, 0)),
            pl.BlockSpec((None, m, d), lambda b, i: (layer, b, 0)),
        ],
        out_specs=pl.BlockSpec((ts, d), lambda b, i: (b * tiles + i, 0)),
        out_shape=jax.ShapeDtypeStruct((t, d), BF16),
        compiler_params=_params("parallel", "arbitrary"),
        name="attention",
    )(q, k, v)


def _even_mixer_kernel(z_ref, zh_ref, ws_ref, bs_ref, glg_ref, glb_ref, cw_ref, cb_ref,
                       clg_ref, clb_ref, y_ref, hh_ref, sh_ref, cv_ref, *, ts, width):
    i = pl.program_id(1)
    hd = width // A_HEADS

    u = jax.nn.gelu(z_ref[:, 0:width])
    v = jax.nn.gelu(z_ref[:, width:2 * width])
    vn = _layer_norm(v, glg_ref[...], glb_ref[...]).astype(BF16)
    row_chunk = lax.broadcasted_iota(jnp.int32, (GMLP_BLOCK, GMLP_BLOCK), 0) // CHUNK
    col_chunk = lax.broadcasted_iota(jnp.int32, (GMLP_BLOCK, GMLP_BLOCK), 1) // CHUNK
    causal = col_chunk <= row_chunk
    nblk = ts // GMLP_BLOCK
    for h in range(A_HEADS):
        cs = slice(h * hd, (h + 1) * hd)
        w = jnp.where(causal, ws_ref[h], 0.0).astype(BF16)
        vh = jnp.concatenate(
            [vn[n * GMLP_BLOCK:(n + 1) * GMLP_BLOCK, cs] for n in range(nblk)], axis=1)
        sp = jnp.dot(w, vh, preferred_element_type=F32) + bs_ref[:, h:h + 1]
        for n in range(nblk):
            rs = slice(n * GMLP_BLOCK, (n + 1) * GMLP_BLOCK)
            y_ref[rs, cs] = (u[rs, cs] * sp[:, n * hd:(n + 1) * hd]).astype(y_ref.dtype)

    hh_ref[CONV_HALO:, :] = z_ref[:, 2 * width:3 * width] * jax.nn.sigmoid(z_ref[:, 3 * width:4 * width])
    halo = zh_ref[:, 0:width] * jax.nn.sigmoid(zh_ref[:, width:2 * width])
    hh_ref[0:CONV_HALO, :] = jnp.where(i == 0, 0.0, halo)
    first = CONV_HALO - (CONV_WIDTH - 1)
    span = ts + CONV_HALO - SUBLANES
    for res in range(1, SUBLANES):
        sh_ref[res - 1] = hh_ref[pl.ds(res, span), :]
    for c in range(width // LANES):
        ls = slice(c * LANES, (c + 1) * LANES)
        acc = None
        for j in range(first, first + CONV_WIDTH):
            res = j % SUBLANES
            if res == 0:
                tap = hh_ref[pl.ds(j, ts), ls]
            else:
                tap = sh_ref[res - 1, pl.ds(j - res, ts), ls]
            term = cw_ref[j - first:j - first + 1, ls] * tap
            acc = term if acc is None else acc + term
        cv_ref[:, ls] = acc + cb_ref[:, ls]
    y_ref[:, width:2 * width] = _silu(_layer_norm(cv_ref[...], clg_ref[...], clb_ref[...])).astype(y_ref.dtype)


def _even_mixer(z, e, w_s, b_s_t, gln_g, gln_b, conv_w, conv_b, cln_g, cln_b, batch, ts):
    _, s, zw = z.shape
    width = zw // 4
    tiles = s // ts
    halo_per_tile = ts // CONV_HALO
    vec = lambda: pl.BlockSpec((None, 1, width), lambda b, i: (e, 0, 0))
    return pl.pallas_call(
        functools.partial(_even_mixer_kernel, ts=ts, width=width),
        grid=(batch, tiles),
        in_specs=[
            pl.BlockSpec((None, ts, zw), lambda b, i: (b, i, 0)),
            pl.BlockSpec((None, CONV_HALO, 2 * width),
                         lambda b, i: (b, jnp.maximum(i * halo_per_tile - 1, 0), 1)),
            pl.BlockSpec((None, A_HEADS, GMLP_BLOCK, GMLP_BLOCK), lambda b, i: (e, 0, 0, 0)),
            pl.BlockSpec((None, GMLP_BLOCK, A_HEADS), lambda b, i: (e, 0, 0)),
            vec(), vec(),
            pl.BlockSpec((None, CONV_WIDTH, width), lambda b, i: (e, 0, 0)),
            vec(), vec(), vec(),
        ],
        out_specs=pl.BlockSpec((None, ts, 2 * width), lambda b, i: (b, i, 0)),
        out_shape=jax.ShapeDtypeStruct((batch, s, 2 * width), BF16),
        scratch_shapes=[
            pltpu.VMEM((ts + CONV_HALO, width), F32),
            pltpu.VMEM((SUBLANES - 1, ts + CONV_HALO - SUBLANES, width), F32),
            pltpu.VMEM((ts, width), F32),
        ],
        compiler_params=_params("parallel", "arbitrary"),
        name="even_mixer",
    )(z, z, w_s, b_s_t, gln_g, gln_b, conv_w, conv_b, cln_g, cln_b)


def _pool_mixer_kernel(x_ref, xh_ref, g_ref, w_ref, b_ref, sc_ref, gn_ref, o_ref, hn_ref, *, ts, group_dim):
    i = pl.program_id(1)
    g = g_ref[...]
    x = x_ref[...]
    h = _rms(x, g)
    h_halo = jnp.where(i == 0, 0.0, _rms(xh_ref[...], g))
    hist = jnp.concatenate([h_halo, h], axis=0)
    frames = i * ts + lax.broadcasted_iota(jnp.int32, (ts, 1), 0) + 1
    for gi, win in enumerate(POOL_WINDOWS):
        cs = slice(gi * group_dim, (gi + 1) * group_dim)
        wsum = hist[:, cs]
        span = 1
        while span < win:
            wsum = wsum + pltpu.roll(wsum, span, axis=0)
            span *= 2
        mean = wsum[POOL_HALO:, :] / jnp.minimum(frames, win).astype(F32)
        d = (mean - h[:, cs]).astype(BF16)
        out = jnp.dot(d, w_ref[gi].astype(BF16), preferred_element_type=F32) + b_ref[:, cs]
        o_ref[:, cs] = x[:, cs] + out * sc_ref[:, cs]
    hn_ref[...] = _rms(o_ref[...], gn_ref[...]).astype(BF16)


def _pool_mixer(x, gain, o, pool_w, pool_b, pool_scale, next_gain, batch, ts):
    _, s, d = x.shape
    g_arr, gl = gain
    gn_arr, gnl = next_gain
    groups = len(POOL_WINDOWS)
    group_dim = d // groups
    tiles = s // ts
    halo_per_tile = ts // POOL_HALO
    tile = lambda: pl.BlockSpec((None, ts, d), lambda b, i: (b, i, 0))
    return pl.pallas_call(
        functools.partial(_pool_mixer_kernel, ts=ts, group_dim=group_dim),
        grid=(batch, tiles),
        in_specs=[
            tile(),
            pl.BlockSpec((None, POOL_HALO, d), lambda b, i: (b, jnp.maximum(i * halo_per_tile - 1, 0), 0)),
            pl.BlockSpec((None, 1, d), lambda b, i: (gl, 0, 0)),
            pl.BlockSpec((None, groups, group_dim, group_dim), lambda b, i: (o, 0, 0, 0)),
            pl.BlockSpec((None, 1, d), lambda b, i: (o, 0, 0)),
            pl.BlockSpec((None, 1, d), lambda b, i: (o, 0, 0)),
            pl.BlockSpec((None, 1, d), lambda b, i: (gnl, 0, 0)),
        ],
        out_specs=[tile(), tile()],
        out_shape=[jax.ShapeDtypeStruct(x.shape, F32), jax.ShapeDtypeStruct(x.shape, BF16)],
        compiler_params=_params("parallel", "arbitrary"),
        name="pool_mixer",
    )(x, x, g_arr, pool_w, pool_b, pool_scale, gn_arr)


def kernel(x, mem, norm_ffn1, ffn1_gate, ffn1_up, ffn1_down, norm_mix, ab_w_in, ab_b_in, gmlp_w_s, gmlp_b_s, gmlp_ln_g, gmlp_ln_b, conv_w, conv_b, conv_ln_g, conv_ln_b, ab_w_out, ab_b_out, pool_w, pool_b, pool_scale, norm_xq, norm_xkv, xattn_wq, xattn_wk, xattn_wv, xattn_wo, norm_ffn2, ffn2_gate, ffn2_up, ffn2_down, norm_final):
    batch, seq, d = x.shape
    t = batch * seq
    row = lambda a: a.reshape(a.shape[0], 1, a.shape[-1])

    norm_ffn1, norm_mix, norm_xq, norm_xkv, norm_ffn2 = map(row, (norm_ffn1, norm_mix, norm_xq, norm_xkv, norm_ffn2))
    ab_b_in, ab_b_out, pool_scale = row(ab_b_in), row(ab_b_out), row(pool_scale)
    pool_b = pool_b.reshape(pool_b.shape[0], 1, d)
    gmlp_b_s_t = jnp.swapaxes(gmlp_b_s, 1, 2)
    gmlp_ln_g, gmlp_ln_b, conv_b, conv_ln_g, conv_ln_b = map(row, (gmlp_ln_g, gmlp_ln_b, conv_b, conv_ln_g, conv_ln_b))
    conv_w = conv_w.reshape(conv_w.shape[0], CONV_WIDTH, conv_w.shape[-1])
    xf = x.reshape(t, d)
    k_all, v_all, h1 = _memory_kv(mem.reshape(batch * N_MEM, d), norm_xkv, xattn_wk, xattn_wv,
                                  xf, norm_ffn1, tn=SWIGLU_COLS)

    norm_final = norm_final.reshape(1, 1, d)
    for l in range(DEPTH):
        hidden, w_down = _rows_matmul(h1, [(ffn1_gate, l), (ffn1_up, l)], None, BF16, SWIGLU_COLS,
                                      swiglu=True, round_weight=(ffn1_down, l))
        if l % 2 == 0:
            e = l // 2
            xf, hm = _ffn_down(hidden, w_down, xf, (norm_mix, l), "bf16", ROW_TILE_F32)
            z, = _rows_matmul(hm, [(ab_w_in, e)], (ab_b_in, e), F32, PROJ_COLS)
            y = _even_mixer(z.reshape(batch, seq, -1), e, gmlp_w_s, gmlp_b_s_t, gmlp_ln_g, gmlp_ln_b,
                            conv_w, conv_b, conv_ln_g, conv_ln_b, batch, ROW_TILE_F32)
            xf, hq = _proj_residual(y.reshape(t, d), (ab_w_out, e), (ab_b_out, e), xf, (norm_xq, l), ROW_TILE_F32)
        else:
            o = l // 2
            xf, = _ffn_down(hidden, w_down, xf, None, "none", ROW_TILE_F32)
            xf, hq = _pool_mixer(xf.reshape(batch, seq, d), (norm_mix, l), o, pool_w, pool_b, pool_scale,
                                 (norm_xq, l), batch, ROW_TILE_F32)
            xf, hq = xf.reshape(t, d), hq.reshape(t, d)
        q, = _rows_matmul(hq, [(xattn_wq, l)], None, BF16, PROJ_COLS)
        att = _attention(q, k_all, v_all, l, batch, ts=seq)
        xf, hf = _proj_residual(att, (xattn_wo, l), None, xf, (norm_ffn2, l), ROW_TILE_F32)
        hidden, w_down = _rows_matmul(hf, [(ffn2_gate, l), (ffn2_up, l)], None, BF16, SWIGLU_COLS,
                                      swiglu=True, round_weight=(ffn2_down, l))
        if l + 1 < DEPTH:
            xf, h1 = _ffn_down(hidden, w_down, xf, (norm_ffn1, l + 1), "bf16", ROW_TILE_F32)
        else:
            out, = _ffn_down(hidden, w_down, xf, (norm_final, 0), "final", ROW_TILE_F32)
    return out.reshape(batch, seq, d)
```

```python
import functools

import jax
import jax.numpy as jnp
from jax import lax
from jax.experimental import pallas as pl
from jax.experimental.pallas import tpu as pltpu

F32 = jnp.float32
BF16 = jnp.bfloat16

DEPTH = 4
CHUNK = 64
N_MEM = 256
N_MEM_HEADS = 4
A_HEADS = 8
GMLP_BLOCK = 128
CONV_WIDTH = 31
POOL_WINDOWS = (2, 4, 8, 16)
EPS = 1e-6

LANES = 128
SUBLANES = 8
BF16_SUBLANES = 16
MXU_COLS = 256
MATMUL_ROW_CHUNK = 1024
ATTN_ROW_CHUNK = 512
FFN_DOWN_COLS = 512
PROJ_ROW_PIECES = 2
CONV_HALO = 32
POOL_HALO = 16

ROW_TILE_BF16 = 2048
ROW_TILE_F32 = 512
SWIGLU_COLS = 512
PROJ_COLS = 1024

VMEM_LIMIT_BYTES = 58 * 1024 * 1024


def _params(*semantics):
    return pltpu.CompilerParams(dimension_semantics=semantics, vmem_limit_bytes=VMEM_LIMIT_BYTES)


def _rms(x, g):
    return x * lax.rsqrt(jnp.mean(x * x, axis=-1, keepdims=True) + EPS) * g


def _layer_norm(x, g, b):
    mu = jnp.mean(x, axis=-1, keepdims=True)
    xc = x - mu
    var = jnp.mean(xc * xc, axis=-1, keepdims=True)
    return xc * lax.rsqrt(var + EPS) * g + b


def _silu(x):
    return x * jax.nn.sigmoid(x)


def _rows_matmul_kernel(*refs, n_w, has_bias, normalized, swiglu, row_chunk, side_job):
    refs = list(refs)
    x_ref = refs.pop(0)
    g_ref = None if normalized else refs.pop(0)
    w_refs = [refs.pop(0) for _ in range(n_w)]
    b_ref = refs.pop(0) if has_bias else None
    side_in = [refs.pop(0) for _ in range({None: 0, "round": 1, "rmsnorm": 2}[side_job])]
    o_refs = [refs.pop(0) for _ in range(1 if swiglu else n_w)]
    if side_job == "round":
        refs.pop(0)[...] = side_in[0][...].astype(BF16)
    elif side_job == "rmsnorm":
        refs.pop(0)[...] = _rms(side_in[0][...], side_in[1][...]).astype(BF16)

    if normalized:
        h_ref = x_ref
    else:
        h_ref, = refs

        @pl.when(pl.program_id(1) == 0)
        def _():
            h_ref[...] = _rms(x_ref[...], g_ref[...]).astype(BF16)

    tm, tn = o_refs[0].shape
    for c in range(tn // MXU_COLS):
        cs = slice(c * MXU_COLS, (c + 1) * MXU_COLS)
        w_cols = [w_ref[:, cs].astype(BF16) for w_ref in w_refs]
        for r in range(tm // row_chunk):
            rs = slice(r * row_chunk, (r + 1) * row_chunk)
            accs = [jnp.dot(h_ref[rs, :], w, preferred_element_type=F32) for w in w_cols]
            if swiglu:
                o_refs[0][rs, cs] = (_silu(accs[0]) * accs[1]).astype(o_refs[0].dtype)
            else:
                for acc, o_ref in zip(accs, o_refs):
                    if has_bias:
                        acc = acc + b_ref[:, cs]
                    o_ref[rs, cs] = acc.astype(o_ref.dtype)


def _rows_matmul(h, ws, bias, out_dtype, tn, swiglu=False, round_weight=None):
    t, k = h.shape
    n = ws[0][0].shape[-1]
    tm = ROW_TILE_BF16
    steps_j = n // tn
    in_specs = [pl.BlockSpec((tm, k), lambda i, j: (i, 0))]
    args = [h]
    for w_arr, wl in ws:
        in_specs.append(pl.BlockSpec((None, k, tn), lambda i, j, wl=wl: (wl, 0, j)))
        args.append(w_arr)
    if bias is not None:
        b_arr, bl = bias
        in_specs.append(pl.BlockSpec((None, 1, tn), lambda i, j: (bl, 0, j)))
        args.append(b_arr)
    n_out = 1 if swiglu else len(ws)
    out_specs = [pl.BlockSpec((tm, tn), lambda i, j: (i, j)) for _ in range(n_out)]
    out_shape = [jax.ShapeDtypeStruct((t, n), out_dtype) for _ in range(n_out)]
    if round_weight is not None:
        wide, wide_l = round_weight
        _, wide_rows, wide_cols = wide.shape
        slab = wide_rows // ((t // tm) * steps_j)
        assert slab * (t // tm) * steps_j == wide_rows and slab % BF16_SUBLANES == 0
        in_specs.append(pl.BlockSpec((None, slab, wide_cols), lambda i, j: (wide_l, i * steps_j + j, 0)))
        args.append(wide)
        out_specs.append(pl.BlockSpec((slab, wide_cols), lambda i, j: (i * steps_j + j, 0)))
        out_shape.append(jax.ShapeDtypeStruct((wide_rows, wide_cols), BF16))
    return pl.pallas_call(
        functools.partial(_rows_matmul_kernel, n_w=len(ws), has_bias=bias is not None,
                          normalized=True, swiglu=swiglu, row_chunk=MATMUL_ROW_CHUNK,
                          side_job=None if round_weight is None else "round"),
        grid=(t // tm, steps_j),
        in_specs=in_specs,
        out_specs=out_specs,
        out_shape=out_shape,
        compiler_params=_params("parallel", "arbitrary"),
        name="swiglu" if swiglu else "rows_matmul",
    )(*args)


def _memory_kv(mem, gain, wk, wv, x, x_gain, tn):
    m, k = mem.shape
    layers, _, n = wk.shape
    steps_j = n // tn
    t = x.shape[0]
    slab = t // (layers * steps_j)
    assert slab * layers * steps_j == t and slab % BF16_SUBLANES == 0
    w_spec = lambda: pl.BlockSpec((None, k, tn), lambda l, j: (l, 0, j))
    o_spec = lambda: pl.BlockSpec((None, m, tn), lambda l, j: (l, 0, j))
    x_spec = lambda: pl.BlockSpec((slab, k), lambda l, j: (l * steps_j + j, 0))
    return pl.pallas_call(
        functools.partial(_rows_matmul_kernel, n_w=2, has_bias=False, normalized=False, swiglu=False,
                          row_chunk=min(m, MATMUL_ROW_CHUNK), side_job="rmsnorm"),
        grid=(layers, steps_j),
        in_specs=[
            pl.BlockSpec((m, k), lambda l, j: (0, 0)),
            pl.BlockSpec((None, 1, k), lambda l, j: (l, 0, 0)),
            w_spec(), w_spec(),
            x_spec(),
            pl.BlockSpec((None, 1, k), lambda l, j: (0, 0, 0)),
        ],
        out_specs=[o_spec(), o_spec(), x_spec()],
        out_shape=[jax.ShapeDtypeStruct((layers, m, n), BF16)] * 2 + [jax.ShapeDtypeStruct((t, k), BF16)],
        scratch_shapes=[pltpu.VMEM((m, k), BF16)],
        compiler_params=_params("parallel", "arbitrary"),
        name="memory_kv",
    )(mem, gain, wk, wv, x, x_gain)


def _ffn_down_kernel(*refs, emit):
    if emit == "none":
        a_ref, w_ref, r_ref, o_ref = refs
    elif emit == "bf16":
        a_ref, w_ref, r_ref, g_ref, o_ref, hn_ref = refs
    else:
        a_ref, w_ref, r_ref, g_ref, o_ref = refs
    a = a_ref[...]
    for c in range(o_ref.shape[1] // FFN_DOWN_COLS):
        cs = slice(c * FFN_DOWN_COLS, (c + 1) * FFN_DOWN_COLS)
        o_ref[:, cs] = r_ref[:, cs] + 0.5 * jnp.dot(a, w_ref[:, cs], preferred_element_type=F32)
    if emit == "bf16":
        hn_ref[...] = _rms(o_ref[...], g_ref[...]).astype(BF16)
    elif emit == "final":
        o_ref[...] = _rms(o_ref[...], g_ref[...])


def _ffn_down(a, w, res, gain, emit, tm):
    t, k = a.shape
    n = w.shape[-1]
    rows = lambda: pl.BlockSpec((tm, n), lambda i: (i, 0))
    in_specs = [
        pl.BlockSpec((tm, k), lambda i: (i, 0)),
        pl.BlockSpec((k, n), lambda i: (0, 0), pipeline_mode=pl.Buffered(1)),
        rows(),
    ]
    args = [a, w, res]
    if emit != "none":
        g_arr, gl = gain
        in_specs.append(pl.BlockSpec((None, 1, n), lambda i: (gl, 0, 0)))
        args.append(g_arr)
    out_specs = [rows()]
    out_shape = [jax.ShapeDtypeStruct((t, n), F32)]
    if emit == "bf16":
        out_specs.append(rows())
        out_shape.append(jax.ShapeDtypeStruct((t, n), BF16))
    return pl.pallas_call(
        functools.partial(_ffn_down_kernel, emit=emit),
        grid=(t // tm,),
        in_specs=in_specs,
        out_specs=out_specs,
        out_shape=out_shape,
        compiler_params=_params("parallel"),
        name="ffn_down",
    )(*args)


def _proj_residual_kernel(*refs, has_bias):
    if has_bias:
        a_ref, w_ref, b_ref, r_ref, g_ref, o_ref, hn_ref, wb_ref = refs
    else:
        a_ref, w_ref, r_ref, g_ref, o_ref, hn_ref, wb_ref = refs

    @pl.when(pl.program_id(0) == 0)
    def _():
        wb_ref[...] = w_ref[...].astype(BF16)

    tm, n = o_ref.shape
    for r in range(PROJ_ROW_PIECES):
        rs = slice(r * tm // PROJ_ROW_PIECES, (r + 1) * tm // PROJ_ROW_PIECES)
        for c in range(n // FFN_DOWN_COLS):
            cs = slice(c * FFN_DOWN_COLS, (c + 1) * FFN_DOWN_COLS)
            acc = jnp.dot(a_ref[rs, :], wb_ref[:, cs], preferred_element_type=F32)
            if has_bias:
                acc = acc + b_ref[:, cs]
            o_ref[rs, cs] = r_ref[rs, cs] + acc
        hn_ref[rs, :] = _rms(o_ref[rs, :], g_ref[...]).astype(BF16)


def _proj_residual(a, w, bias, res, next_gain, tm):
    t, k = a.shape
    w_arr, wl = w
    g_arr, gl = next_gain
    n = w_arr.shape[-1]
    in_specs = [
        pl.BlockSpec((tm, k), lambda i: (i, 0)),
        pl.BlockSpec((None, k, n), lambda i: (wl, 0, 0), pipeline_mode=pl.Buffered(1)),
    ]
    args = [a, w_arr]
    if bias is not None:
        b_arr, bl = bias
        in_specs.append(pl.BlockSpec((None, 1, n), lambda i: (bl, 0, 0)))
        args.append(b_arr)
    in_specs += [pl.BlockSpec((tm, n), lambda i: (i, 0)), pl.BlockSpec((None, 1, n), lambda i: (gl, 0, 0))]
    args += [res, g_arr]
    return pl.pallas_call(
        functools.partial(_proj_residual_kernel, has_bias=bias is not None),
        grid=(t // tm,),
        in_specs=in_specs,
        out_specs=[pl.BlockSpec((tm, n), lambda i: (i, 0)), pl.BlockSpec((tm, n), lambda i: (i, 0))],
        out_shape=[jax.ShapeDtypeStruct((t, n), F32), jax.ShapeDtypeStruct((t, n), BF16)],
        scratch_shapes=[pltpu.VMEM((k, n), BF16)],
        compiler_params=_params("arbitrary"),
        name="proj_residual",
    )(*args)


def _attn_kernel(q_ref, k_ref, v_ref, o_ref, *, head_dim):
    scale = head_dim ** -0.5
    for r in range(q_ref.shape[0] // ATTN_ROW_CHUNK):
        rs = slice(r * ATTN_ROW_CHUNK, (r + 1) * ATTN_ROW_CHUNK)
        for h in range(N_MEM_HEADS):
            sl = slice(h * head_dim, (h + 1) * head_dim)
            s = lax.dot_general(q_ref[rs, sl], k_ref[:, sl], (((1,), (1,)), ((), ())),
                                preferred_element_type=F32) * scale
            e = jnp.exp(s - jnp.max(s, axis=-1, keepdims=True))
            p = e / jnp.sum(e, axis=-1, keepdims=True)
            o_ref[rs, sl] = jnp.dot(p.astype(BF16), v_ref[:, sl],
                                    preferred_element_type=F32).astype(o_ref.dtype)


def _attention(q, k, v, layer, batch, ts):
    t, d = q.shape
    m = k.shape[1] // batch
    tiles = t // batch // ts
    return pl.pallas_call(
        functools.partial(_attn_kernel, head_dim=d // N_MEM_HEADS),
        grid=(batch, tiles),
        in_specs=[
            pl.BlockSpec((ts, d), lambda b, i: (b * tiles + i, 0)),
            pl.BlockSpec((None, m, d), lambda b, i: (layer, b, 0)),
            pl.BlockSpec((None, m, d), lambda b, i: (layer, b, 0)),
        ],
        out_specs=pl.BlockSpec((ts, d), lambda b, i: (b * tiles + i, 0)),
        out_shape=jax.ShapeDtypeStruct((t, d), BF16),
        compiler_params=_params("parallel", "arbitrary"),
        name="attention",
    )(q, k, v)


def _even_mixer_kernel(z_ref, zh_ref, ws_ref, bs_ref, glg_ref, glb_ref, cw_ref, cb_ref,
                       clg_ref, clb_ref, y_ref, hh_ref, sh_ref, cv_ref, *, ts, width):
    i = pl.program_id(1)
    hd = width // A_HEADS

    u = jax.nn.gelu(z_ref[:, 0:width])
    v = jax.nn.gelu(z_ref[:, width:2 * width])
    vn = _layer_norm(v, glg_ref[...], glb_ref[...]).astype(BF16)
    row_chunk = lax.broadcasted_iota(jnp.int32, (GMLP_BLOCK, GMLP_BLOCK), 0) // CHUNK
    col_chunk = lax.broadcasted_iota(jnp.int32, (GMLP_BLOCK, GMLP_BLOCK), 1) // CHUNK
    causal = col_chunk <= row_chunk
    nblk = ts // GMLP_BLOCK
    for h in range(A_HEADS):
        cs = slice(h * hd, (h + 1) * hd)
        w = jnp.where(causal, ws_ref[h], 0.0).astype(BF16)
        vh = jnp.concatenate(
            [vn[n * GMLP_BLOCK:(n + 1) * GMLP_BLOCK, cs] for n in range(nblk)], axis=1)
        sp = jnp.dot(w, vh, preferred_element_type=F32) + bs_ref[:, h:h + 1]
        for n in range(nblk):
            rs = slice(n * GMLP_BLOCK, (n + 1) * GMLP_BLOCK)
            y_ref[rs, cs] = (u[rs, cs] * sp[:, n * hd:(n + 1) * hd]).astype(y_ref.dtype)

    hh_ref[CONV_HALO:, :] = z_ref[:, 2 * width:3 * width] * jax.nn.sigmoid(z_ref[:, 3 * width:4 * width])
    halo = zh_ref[:, 0:width] * jax.nn.sigmoid(zh_ref[:, width:2 * width])
    hh_ref[0:CONV_HALO, :] = jnp.where(i == 0, 0.0, halo)
    first = CONV_HALO - (CONV_WIDTH - 1)
    span = ts + CONV_HALO - SUBLANES
    for res in range(1, SUBLANES):
        sh_ref[res - 1] = hh_ref[pl.ds(res, span), :]
    for c in range(width // LANES):
        ls = slice(c * LANES, (c + 1) * LANES)
        acc = None
        for j in range(first, first + CONV_WIDTH):
            res = j % SUBLANES
            if res == 0:
                tap = hh_ref[pl.ds(j, ts), ls]
            else:
                tap = sh_ref[res - 1, pl.ds(j - res, ts), ls]
            term = cw_ref[j - first:j - first + 1, ls] * tap
            acc = term if acc is None else acc + term
        cv_ref[:, ls] = acc + cb_ref[:, ls]
    y_ref[:, width:2 * width] = _silu(_layer_norm(cv_ref[...], clg_ref[...], clb_ref[...])).astype(y_ref.dtype)


def _even_mixer(z, e, w_s, b_s_t, gln_g, gln_b, conv_w, conv_b, cln_g, cln_b, batch, ts):
    _, s, zw = z.shape
    width = zw // 4
    tiles = s // ts
    halo_per_tile = ts // CONV_HALO
    vec = lambda: pl.BlockSpec((None, 1, width), lambda b, i: (e, 0, 0))
    return pl.pallas_call(
        functools.partial(_even_mixer_kernel, ts=ts, width=width),
        grid=(batch, tiles),
        in_specs=[
            pl.BlockSpec((None, ts, zw), lambda b, i: (b, i, 0)),
            pl.BlockSpec((None, CONV_HALO, 2 * width),
                         lambda b, i: (b, jnp.maximum(i * halo_per_tile - 1, 0), 1)),
            pl.BlockSpec((None, A_HEADS, GMLP_BLOCK, GMLP_BLOCK), lambda b, i: (e, 0, 0, 0)),
            pl.BlockSpec((None, GMLP_BLOCK, A_HEADS), lambda b, i: (e, 0, 0)),
            vec(), vec(),
            pl.BlockSpec((None, CONV_WIDTH, width), lambda b, i: (e, 0, 0)),
            vec(), vec(), vec(),
        ],
        out_specs=pl.BlockSpec((None, ts, 2 * width), lambda b, i: (b, i, 0)),
        out_shape=jax.ShapeDtypeStruct((batch, s, 2 * width), BF16),
        scratch_shapes=[
            pltpu.VMEM((ts + CONV_HALO, width), F32),
            pltpu.VMEM((SUBLANES - 1, ts + CONV_HALO - SUBLANES, width), F32),
            pltpu.VMEM((ts, width), F32),
        ],
        compiler_params=_params("parallel", "arbitrary"),
        name="even_mixer",
    )(z, z, w_s, b_s_t, gln_g, gln_b, conv_w, conv_b, cln_g, cln_b)


def _pool_mixer_kernel(x_ref, xh_ref, g_ref, w_ref, b_ref, sc_ref, gn_ref, o_ref, hn_ref, *, ts, group_dim):
    i = pl.program_id(1)
    g = g_ref[...]
    x = x_ref[...]
    h = _rms(x, g)
    h_halo = jnp.where(i == 0, 0.0, _rms(xh_ref[...], g))
    hist = jnp.concatenate([h_halo, h], axis=0)
    frames = i * ts + lax.broadcasted_iota(jnp.int32, (ts, 1), 0) + 1
    for gi, win in enumerate(POOL_WINDOWS):
        cs = slice(gi * group_dim, (gi + 1) * group_dim)
        wsum = hist[:, cs]
        span = 1
        while span < win:
            wsum = wsum + pltpu.roll(wsum, span, axis=0)
            span *= 2
        mean = wsum[POOL_HALO:, :] / jnp.minimum(frames, win).astype(F32)
        d = (mean - h[:, cs]).astype(BF16)
        out = jnp.dot(d, w_ref[gi].astype(BF16), preferred_element_type=F32) + b_ref[:, cs]
        o_ref[:, cs] = x[:, cs] + out * sc_ref[:, cs]
    hn_ref[...] = _rms(o_ref[...], gn_ref[...]).astype(BF16)


def _pool_mixer(x, gain, o, pool_w, pool_b, pool_scale, next_gain, batch, ts):
    _, s, d = x.shape
    g_arr, gl = gain
    gn_arr, gnl = next_gain
    groups = len(POOL_WINDOWS)
    group_dim = d // groups
    tiles = s // ts
    halo_per_tile = ts // POOL_HALO
    tile = lambda: pl.BlockSpec((None, ts, d), lambda b, i: (b, i, 0))
    return pl.pallas_call(
        functools.partial(_pool_mixer_kernel, ts=ts, group_dim=group_dim),
        grid=(batch, tiles),
        in_specs=[
            tile(),
            pl.BlockSpec((None, POOL_HALO, d), lambda b, i: (b, jnp.maximum(i * halo_per_tile - 1, 0), 0)),
            pl.BlockSpec((None, 1, d), lambda b, i: (gl, 0, 0)),
            pl.BlockSpec((None, groups, group_dim, group_dim), lambda b, i: (o, 0, 0, 0)),
            pl.BlockSpec((None, 1, d), lambda b, i: (o, 0, 0)),
            pl.BlockSpec((None, 1, d), lambda b, i: (o, 0, 0)),
            pl.BlockSpec((None, 1, d), lambda b, i: (gnl, 0, 0)),
        ],
        out_specs=[tile(), tile()],
        out_shape=[jax.ShapeDtypeStruct(x.shape, F32), jax.ShapeDtypeStruct(x.shape, BF16)],
        compiler_params=_params("parallel", "arbitrary"),
        name="pool_mixer",
    )(x, x, g_arr, pool_w, pool_b, pool_scale, gn_arr)


def kernel(x, mem, norm_ffn1, ffn1_gate, ffn1_up, ffn1_down, norm_mix, ab_w_in, ab_b_in, gmlp_w_s, gmlp_b_s, gmlp_ln_g, gmlp_ln_b, conv_w, conv_b, conv_ln_g, conv_ln_b, ab_w_out, ab_b_out, pool_w, pool_b, pool_scale, norm_xq, norm_xkv, xattn_wq, xattn_wk, xattn_wv, xattn_wo, norm_ffn2, ffn2_gate, ffn2_up, ffn2_down, norm_final):
    batch, seq, d = x.shape
    t = batch * seq
    row = lambda a: a.reshape(a.shape[0], 1, a.shape[-1])

    norm_ffn1, norm_mix, norm_xq, norm_xkv, norm_ffn2 = map(row, (norm_ffn1, norm_mix, norm_xq, norm_xkv, norm_ffn2))
    ab_b_in, ab_b_out, pool_scale = row(ab_b_in), row(ab_b_out), row(pool_scale)
    pool_b = pool_b.reshape(pool_b.shape[0], 1, d)
    gmlp_b_s_t = jnp.swapaxes(gmlp_b_s, 1, 2)
    gmlp_ln_g, gmlp_ln_b, conv_b, conv_ln_g, conv_ln_b = map(row, (gmlp_ln_g, gmlp_ln_b, conv_b, conv_ln_g, conv_ln_b))
    conv_w = conv_w.reshape(conv_w.shape[0], CONV_WIDTH, conv_w.shape[-1])
    xf = x.reshape(t, d)
    k_all, v_all, h1 = _memory_kv(mem.reshape(batch * N_MEM, d), norm_xkv, xattn_wk, xattn_wv,
                                  xf, norm_ffn1, tn=SWIGLU_COLS)

    norm_final = norm_final.reshape(1, 1, d)
    for l in range(DEPTH):
        hidden, w_down = _rows_matmul(h1, [(ffn1_gate, l), (ffn1_up, l)], None, BF16, SWIGLU_COLS,
                                      swiglu=True, round_weight=(ffn1_down, l))
        if l % 2 == 0:
            e = l // 2
            xf, hm = _ffn_down(hidden, w_down, xf, (norm_mix, l), "bf16", ROW_TILE_F32)
            z, = _rows_matmul(hm, [(ab_w_in, e)], (ab_b_in, e), F32, PROJ_COLS)
            y = _even_mixer(z.reshape(batch, seq, -1), e, gmlp_w_s, gmlp_b_s_t, gmlp_ln_g, gmlp_ln_b,
                            conv_w, conv_b, conv_ln_g, conv_ln_b, batch, ROW_TILE_F32)
            xf, hq = _proj_residual(y.reshape(t, d), (ab_w_out, e), (ab_b_out, e), xf, (norm_xq, l), ROW_TILE_F32)
        else:
            o = l // 2
            xf, = _ffn_down(hidden, w_down, xf, None, "none", ROW_TILE_F32)
            xf, hq = _pool_mixer(xf.reshape(batch, seq, d), (norm_mix, l), o, pool_w, pool_b, pool_scale,
                                 (norm_xq, l), batch, ROW_TILE_F32)
            xf, hq = xf.reshape(t, d), hq.reshape(t, d)
        q, = _rows_matmul(hq, [(xattn_wq, l)], None, BF16, PROJ_COLS)
        att = _attention(q, k_all, v_all, l, batch, ts=seq)
        xf, hf = _proj_residual(att, (xattn_wo, l), None, xf, (norm_ffn2, l), ROW_TILE_F32)
        hidden, w_down = _rows_matmul(hf, [(ffn2_gate, l), (ffn2_up, l)], None, BF16, SWIGLU_COLS,
                                      swiglu=True, round_weight=(ffn2_down, l))
        if l + 1 < DEPTH:
            xf, h1 = _ffn_down(hidden, w_down, xf, (norm_ffn1, l + 1), "bf16", ROW_TILE_F32)
        else:
            out, = _ffn_down(hidden, w_down, xf, (norm_final, 0), "final", ROW_TILE_F32)
    return out.reshape(batch, seq, d)
```
